```python
import math
import jax, jax.numpy as jnp
from jax import lax
import numpy as np


D_MODEL = 1024
BATCH = 2
SEQ = 8192
DEPTH = 2

N_META = 16
MIX_WIDTH = D_MODEL
DIFF_WIDTH = MIX_WIDTH // 2
FOX_WIDTH = MIX_WIDTH - DIFF_WIDTH
DIFF_QK_DIM = 64
DIFF_V_DIM = 2 * DIFF_QK_DIM
DIFF_HEADS = DIFF_WIDTH // DIFF_V_DIM
FOX_HEAD_DIM = 64
FOX_HEADS = FOX_WIDTH // FOX_HEAD_DIM
ROPE_THETA = 10000.0
Q_BLOCK = 128
NORM_EPS = 1e-6
SPLITS = (DIFF_HEADS * 2 * DIFF_QK_DIM, DIFF_HEADS * 2 * DIFF_QK_DIM, DIFF_WIDTH, DIFF_WIDTH,
          FOX_WIDTH, FOX_WIDTH, FOX_WIDTH, FOX_WIDTH, FOX_HEADS)
PROJ_WIDTH = sum(SPLITS)

kernel_name = 'hymba_diff_fox_hybrid'


def rms_norm(x, g):
    xf = x.astype(jnp.float32)
    y = xf * lax.rsqrt(jnp.mean(xf * xf, axis=-1, keepdims=True) + NORM_EPS)
    return (y * g.astype(jnp.float32)).astype(x.dtype)


def rope(t, pos):
    half = t.shape[-1] // 2
    inv = ROPE_THETA ** (-jnp.arange(half, dtype=jnp.float32) / half)
    ang = pos.astype(jnp.float32)[:, None] * inv[None, :]
    cos, sin = jnp.cos(ang), jnp.sin(ang)
    tf = t.astype(jnp.float32)
    t1, t2 = tf[..., :half], tf[..., half:]
    return jnp.concatenate([t1 * cos - t2 * sin, t2 * cos + t1 * sin], axis=-1).astype(t.dtype)


def sweep_query_blocks(block_fn, q_parts, pos):
    meta_out = block_fn(tuple(q[:, :, :N_META] for q in q_parts), pos[:N_META])

    def to_blocks(q):
        b, h, _, d = q.shape
        return jnp.moveaxis(q[:, :, N_META:].reshape(b, h, -1, Q_BLOCK, d), 2, 0)

    blocks = tuple(to_blocks(q) for q in q_parts)
    pos_blocks = pos[N_META:].reshape(-1, Q_BLOCK)
    outs = lax.map(lambda a: block_fn(a[0], a[1]), (blocks, pos_blocks))
    nb, b, h, t, dv = outs.shape
    real = jnp.moveaxis(outs, 0, 2).reshape(b, h, nb * t, dv)
    return jnp.concatenate([meta_out, real], axis=2)


def hybrid_layer(h, pos, norm_g, w_in, b_forget, lam_q1, lam_k1, lam_q2, lam_k2,
                 subln_g, w_out, lambda_init):
    bsz, length, _ = h.shape
    u = rms_norm(h, norm_g)
    proj = jnp.einsum('bld,dp->blp', u, w_in)
    split_points = np.cumsum(SPLITS)[:-1].tolist()
    dq, dk, dv, dz, fq, fk, fv, fz, f_logit = jnp.split(proj, split_points, axis=-1)

    def diff_qk(t):
        t = t.reshape(bsz, length, DIFF_HEADS, 2, DIFF_QK_DIM).transpose(3, 0, 2, 1, 4)
        return rope(t[0], pos), rope(t[1], pos)

    q1, q2 = diff_qk(dq)
    k1, k2 = diff_qk(dk)
    v_a = dv.reshape(bsz, length, DIFF_HEADS, DIFF_V_DIM).transpose(0, 2, 1, 3)
    lam = (jnp.exp(jnp.sum(lam_q1.astype(jnp.float32) * lam_k1.astype(jnp.float32)))
           - jnp.exp(jnp.sum(lam_q2.astype(jnp.float32) * lam_k2.astype(jnp.float32)))
           + lambda_init)
    d_scale = DIFF_QK_DIM ** -0.5

    def diff_block(qs, qpos):
        qa, qc = qs
        mask = pos[None, :] <= qpos[:, None]

        def probs(q, k):
            s = jnp.einsum('bhqd,bhkd->bhqk', q, k).astype(jnp.float32) * d_scale
            return jax.nn.softmax(jnp.where(mask, s, -jnp.inf), axis=-1)

        p = probs(qa, k1) - lam * probs(qc, k2)
        return jnp.einsum('bhqk,bhkd->bhqd', p.astype(v_a.dtype), v_a)

    o_a = sweep_query_blocks(diff_block, (q1, q2), pos)
    o_a = rms_norm(o_a, subln_g.reshape(DIFF_HEADS, 1, DIFF_V_DIM)) * (1.0 - lambda_init)
    o_a = o_a.transpose(0, 2, 1, 3).reshape(bsz, length, DIFF_WIDTH)

    def fox_heads(t):
        return t.reshape(bsz, length, FOX_HEADS, FOX_HEAD_DIM).transpose(0, 2, 1, 3)

    q_f, k_f, v_f = fox_heads(fq), fox_heads(fk), fox_heads(fv)
    log_f = jax.nn.log_sigmoid(f_logit.astype(jnp.float32) + b_forget.astype(jnp.float32))
    cum = jnp.cumsum(log_f, axis=1).transpose(0, 2, 1)
    f_scale = FOX_HEAD_DIM ** -0.5

    def fox_block(qs, qpos):
        q, c_q = qs
        mask = pos[None, :] <= qpos[:, None]
        s = (jnp.einsum('bhqd,bhkd->bhqk', q, k_f).astype(jnp.float32) * f_scale
             + c_q - cum[:, :, None, :])
        p = jax.nn.softmax(jnp.where(mask, s, -jnp.inf), axis=-1)
        return jnp.einsum('bhqk,bhkd->bhqd', p.astype(v_f.dtype), v_f)

    o_b = sweep_query_blocks(fox_block, (q_f, cum[..., None]), pos)
    o_b = o_b.transpose(0, 2, 1, 3).reshape(bsz, length, FOX_WIDTH)

    mixed = jnp.concatenate([o_a * jax.nn.silu(dz), o_b * jax.nn.silu(fz)], axis=-1)
    return h + jnp.einsum('blm,md->bld', mixed, w_out)


def setup_inputs(seed: int = 0) -> dict:
    key = jax.random.key(seed)
    ks = jax.random.split(key, 14)
    f32 = jnp.float32
    x = jax.random.normal(ks[0], (BATCH, SEQ, D_MODEL), f32)
    meta_tokens = jax.random.normal(ks[1], (N_META, D_MODEL), f32)
    norm_g = 1.0 + 0.02 * jax.random.normal(ks[2], (DEPTH, D_MODEL), f32)
    w_in = jax.random.normal(ks[3], (DEPTH, D_MODEL, PROJ_WIDTH), f32) * D_MODEL ** -0.5
    b_forget = (jnp.linspace(1.0, 6.0, FOX_HEADS, dtype=f32)[None, :]
                + 0.1 * jax.random.normal(ks[4], (DEPTH, FOX_HEADS), f32))
    lam_q1 = 0.1 * jax.random.normal(ks[5], (DEPTH, DIFF_QK_DIM), f32)
    lam_k1 = 0.1 * jax.random.normal(ks[6], (DEPTH, DIFF_QK_DIM), f32)
    lam_q2 = 0.1 * jax.random.normal(ks[7], (DEPTH, DIFF_QK_DIM), f32)
    lam_k2 = 0.1 * jax.random.normal(ks[8], (DEPTH, DIFF_QK_DIM), f32)
    subln_g = 1.0 + 0.02 * jax.random.normal(ks[9], (DEPTH, DIFF_WIDTH), f32)
    w_out = jax.random.normal(ks[10], (DEPTH, MIX_WIDTH, D_MODEL), f32) * MIX_WIDTH ** -0.5
    final_g = 1.0 + 0.02 * jax.random.normal(ks[11], (D_MODEL,), f32)
    return {'x': x, 'meta_tokens': meta_tokens, 'norm_g': norm_g, 'w_in': w_in,
            'b_forget': b_forget, 'lam_q1': lam_q1, 'lam_k1': lam_k1, 'lam_q2': lam_q2,
            'lam_k2': lam_k2, 'subln_g': subln_g, 'w_out': w_out, 'final_g': final_g}


def reference(x, meta_tokens, norm_g, w_in, b_forget, lam_q1, lam_k1, lam_q2, lam_k2,
              subln_g, w_out, final_g):
    bsz, _, dm = x.shape
    meta = jnp.broadcast_to(meta_tokens.astype(x.dtype)[None], (bsz, N_META, dm))
    h = jnp.concatenate([meta, x], axis=1)
    pos = jnp.arange(h.shape[1], dtype=jnp.int32)
    for layer in range(DEPTH):
        lambda_init = 0.8 - 0.6 * math.exp(-0.3 * layer)
        h = hybrid_layer(h, pos, norm_g[layer], w_in[layer], b_forget[layer],
                         lam_q1[layer], lam_k1[layer], lam_q2[layer], lam_k2[layer],
                         subln_g[layer], w_out[layer], lambda_init)
    return rms_norm(h, final_g)[:, N_META:]
```

```python
import functools
import math

import numpy as np
import jax
import jax.numpy as jnp
from jax import lax
from jax.experimental import pallas as pl
from jax.experimental.pallas import tpu as pltpu

N_META = 16
DIFF_QK_DIM = 64
FOX_HEAD_DIM = 64
FOX_HEADS = 8
UNIT = 128
N_UNITS = 4
ROPE_THETA = 10000.0
NORM_EPS = 1e-6
LOG2E = 1.0 / math.log(2.0)
NEG = -1e30

SEQ_TILE = 512
VMEM_LIMIT = 56 * 1024 * 1024

F32 = jnp.float32
BF16 = jnp.bfloat16


def _proj_kernel(h_ref, g_ref, w_ref, bf_ref, cosq_ref, sinq_ref, cosk_ref, sink_ref, tri_ref,
                 qdt_ref, kd_ref, vdt_ref, qft_ref, kf_ref, vft_ref, zg_ref, cqt_ref, ckrep_ref,
                 carry_ref, *, tl, fox_scale):
    step = pl.program_id(1)

    h = h_ref[...]
    u = h * lax.rsqrt(jnp.mean(h * h, axis=-1, keepdims=True) + NORM_EPS) * g_ref[...]
    ub = u.astype(BF16)

    def proj(lo, width):
        return jnp.dot(ub, w_ref[:, lo:lo + width], preferred_element_type=F32)

    def rope(t, cos_ref, sin_ref):
        return t * cos_ref[...] + pltpu.roll(t, 64, 1) * sin_ref[...]

    width = N_UNITS * UNIT
    dq = proj(0 * width, width)
    for un in range(N_UNITS):
        qdt_ref[un] = rope(dq[:, un * UNIT:(un + 1) * UNIT], cosq_ref, sinq_ref).T.astype(BF16)
    dk = proj(1 * width, width)
    for un in range(N_UNITS):
        kd_ref[un] = rope(dk[:, un * UNIT:(un + 1) * UNIT], cosk_ref, sink_ref).astype(BF16)
    dv = proj(2 * width, width)
    for un in range(N_UNITS):
        vdt_ref[un] = dv[:, un * UNIT:(un + 1) * UNIT].T.astype(BF16)
    fq = proj(3 * width, width) * fox_scale
    for un in range(N_UNITS):
        qft_ref[un] = fq[:, un * UNIT:(un + 1) * UNIT].T.astype(BF16)
    fk = proj(4 * width, width)
    for un in range(N_UNITS):
        kf_ref[un] = fk[:, un * UNIT:(un + 1) * UNIT].astype(BF16)
    fv = proj(5 * width, width)
    for un in range(N_UNITS):
        vft_ref[un] = fv[:, un * UNIT:(un + 1) * UNIT].T.astype(BF16)

    z = proj(6 * width, 2 * width)
    zg_ref[...] = z * (1.0 / (1.0 + jnp.exp(-z)))

    fl = proj(8 * width, UNIT) + bf_ref[...]
    ls = (jnp.minimum(fl, 0.0) - jnp.log1p(jnp.exp(-jnp.abs(fl)))) * LOG2E
    row = lax.broadcasted_iota(jnp.int32, (tl, UNIT), 0)
    ls = jnp.where(row >= jnp.where(step == 0, N_META, tl), 0.0, ls)

    @pl.when(step == 0)
    def _():
        carry_ref[...] = jnp.zeros_like(carry_ref)

    hi = ls.astype(BF16)
    r1 = ls - hi.astype(F32)
    mid = r1.astype(BF16)
    lo = (r1 - mid.astype(F32)).astype(BF16)
    tri = tri_ref[...]
    cum = (jnp.dot(tri, hi, preferred_element_type=F32)
           + jnp.dot(tri, mid, preferred_element_type=F32)
           + jnp.dot(tri, lo, preferred_element_type=F32)) + carry_ref[...]
    carry_ref[...] = cum[tl - 1:tl, :]

    cqt_ref[...] = cum.T[0:FOX_HEADS, :]
    for hh in range(FOX_HEADS):
        ckrep_ref[hh] = jnp.broadcast_to(cum[:, hh:hh + 1], (tl, UNIT))


def _project(h, norm_g, w, b_forget, tabs, tri, *, tl):
    bsz, lp, dm = h.shape
    nl = lp // tl
    pw = w.shape[1]

    def tok(b, i):
        return (i + nl - 1) % nl

    row_spec = lambda width: pl.BlockSpec((None, tl, width), lambda b, i: (b, tok(b, i), 0))
    unit_rows = pl.BlockSpec((None, N_UNITS, tl, UNIT), lambda b, i: (b, 0, tok(b, i), 0))
    unit_cols = pl.BlockSpec((None, N_UNITS, UNIT, tl), lambda b, i: (b, 0, 0, tok(b, i)))
    tab_spec = pl.BlockSpec((tl, UNIT), lambda b, i: (tok(b, i), 0))
    const = lambda shape: pl.BlockSpec(shape, lambda b, i: (0,) * len(shape))

    unit_t = jax.ShapeDtypeStruct((bsz, N_UNITS, UNIT, lp), BF16)
    unit_n = jax.ShapeDtypeStruct((bsz, N_UNITS, lp, UNIT), BF16)
    return pl.pallas_call(
        functools.partial(_proj_kernel, tl=tl, fox_scale=FOX_HEAD_DIM ** -0.5 * LOG2E),
        grid=(bsz, nl),
        in_specs=[row_spec(dm), const((1, dm)), const((dm, pw)), const((1, UNIT)),
                  tab_spec, tab_spec, tab_spec, tab_spec, const((tl, tl))],
        out_specs=[unit_cols, unit_rows, unit_cols, unit_cols, unit_rows, unit_cols,
                   row_spec(2 * N_UNITS * UNIT),
                   pl.BlockSpec((None, FOX_HEADS, tl), lambda b, i: (b, 0, tok(b, i))),
                   pl.BlockSpec((None, FOX_HEADS, tl, UNIT), lambda b, i: (b, 0, tok(b, i), 0))],
        out_shape=[unit_t, unit_n, unit_t, unit_t, unit_n, unit_t,
                   jax.ShapeDtypeStruct((bsz, lp, 2 * N_UNITS * UNIT), F32),
                   jax.ShapeDtypeStruct((bsz, FOX_HEADS, lp), F32),
                   jax.ShapeDtypeStruct((bsz, FOX_HEADS, lp, UNIT), F32)],
        scratch_shapes=[pltpu.VMEM((1, UNIT), F32)],
        compiler_params=pltpu.CompilerParams(
            dimension_semantics=("arbitrary", "arbitrary"), vmem_limit_bytes=VMEM_LIMIT),
        name="proj",
    )(h, norm_g, w, b_forget, *tabs, tri)


def _flash_kernel(*refs, fox, tq, seq, lambda_init):
    if fox:
        qt_ref, k_ref, vt_ref, z_ref, cq_ref, ck_ref, o_ref, m_sc, l_sc, acc_sc = refs
    else:
        qt_ref, k_ref, vt_ref, z_ref, lam_ref, g_ref, o_ref, m_sc, l_sc, acc_sc = refs
    tk = tq
    i = pl.program_id(2)
    nq = pl.num_programs(2) - 1

    feat = lax.broadcasted_iota(jnp.int32, (UNIT, tq), 0)
    second = (feat >= FOX_HEAD_DIM) if fox else ((feat // 32) % 2 == 1)
    qt = qt_ref[...].astype(F32)
    q_parts = (jnp.where(second, 0.0, qt).astype(BF16), jnp.where(second, qt, 0.0).astype(BF16))

    m_sc[...] = jnp.full_like(m_sc, NEG)
    l_sc[...] = jnp.zeros_like(l_sc)
    acc_sc[...] = jnp.zeros_like(acc_sc)

    def step(k_off, k_len, mask):
        kt = k_ref[pl.ds(k_off, k_len), :]
        vtt = vt_ref[:, pl.ds(k_off, k_len)]
        for c in range(2):
            s = jnp.dot(kt, q_parts[c], preferred_element_type=F32)
            if fox:
                ck = ck_ref[c, pl.ds(k_off, k_len), :]
                s = jnp.concatenate(
                    [s[:, b * UNIT:(b + 1) * UNIT] - ck for b in range(tq // UNIT)], axis=1)
            if mask is not None:
                s = jnp.where(mask, s, NEG)
            m_old = m_sc[c]
            m_blk = jnp.max(s, axis=0, keepdims=True)
            if fox:
                cq = cq_ref[c:c + 1, :]
                m_new = jnp.maximum(m_old, m_blk + cq)
                shift = m_new - cq
            else:
                m_new = jnp.maximum(m_old, m_blk)
                shift = m_new
            alpha = jnp.exp2(m_old - m_new)
            p = jnp.exp2(s - shift)
            l_sc[c] = alpha * l_sc[c] + jnp.sum(p, axis=0, keepdims=True)
            acc_sc[c] = alpha * acc_sc[c] + jnp.dot(vtt, p.astype(BF16),
                                                    preferred_element_type=F32)
            m_sc[c] = m_new

    @pl.when(i < nq)
    def _():
        key = lax.broadcasted_iota(jnp.int32, (UNIT, tq), 0)
        step(seq, UNIT, key < N_META)

        def body(j, carry):
            step(pl.multiple_of(j * tk, tk), tk, None)
            return carry

        lax.fori_loop(0, i, body, 0)
        key = lax.broadcasted_iota(jnp.int32, (tk, tq), 0)
        qry = lax.broadcasted_iota(jnp.int32, (tk, tq), 1)
        step(pl.multiple_of(i * tk, tk), tk, key <= qry)

    @pl.when(i == nq)
    def _():
        key = lax.broadcasted_iota(jnp.int32, (UNIT, tq), 0)
        qry = lax.broadcasted_iota(jnp.int32, (UNIT, tq), 1)
        step(seq, UNIT, jnp.logical_and(key <= qry, key < N_META))

    o1 = acc_sc[0] * (1.0 / l_sc[0])
    o2 = acc_sc[1] * (1.0 / l_sc[1])
    if fox:
        o = jnp.concatenate([o1[:FOX_HEAD_DIM], o2[FOX_HEAD_DIM:]], axis=0).T
    else:
        lam = (jnp.exp(jnp.sum(lam_ref[0:1, :] * lam_ref[1:2, :], axis=-1, keepdims=True))
               - jnp.exp(jnp.sum(lam_ref[2:3, :] * lam_ref[3:4, :], axis=-1, keepdims=True))
               + lambda_init)
        o = (o1 - lam * o2).T
        o = o * lax.rsqrt(jnp.mean(o * o, axis=-1, keepdims=True) + NORM_EPS) * g_ref[...]
        o = o * (1.0 - lambda_init)
    o_ref[...] = (o * z_ref[...]).astype(BF16)


def _flash(qt, k, vt, zg, extra, *, fox, tq, seq, lambda_init):
    bsz, _, lp, _ = k.shape
    nt = lp // tq
    z_off = N_UNITS if fox else 0
    q_spec = pl.BlockSpec((None, None, UNIT, tq), lambda b, u, i: (b, u, 0, i))
    k_spec = pl.BlockSpec((None, None, lp, UNIT), lambda b, u, i: (b, u, 0, 0))
    vt_spec = pl.BlockSpec((None, None, UNIT, lp), lambda b, u, i: (b, u, 0, 0))
    z_spec = pl.BlockSpec((None, tq, UNIT), lambda b, u, i: (b, i, u + z_off))
    if fox:
        extra_specs = [pl.BlockSpec((None, None, 2, tq), lambda b, u, i: (b, u, 0, i)),
                       pl.BlockSpec((None, None, 2, lp, UNIT), lambda b, u, i: (b, u, 0, 0, 0))]
    else:
        extra_specs = [pl.BlockSpec((4, DIFF_QK_DIM), lambda b, u, i: (0, 0)),
                       pl.BlockSpec((None, 1, UNIT), lambda b, u, i: (u, 0, 0))]
    return pl.pallas_call(
        functools.partial(_flash_kernel, fox=fox, tq=tq, seq=seq, lambda_init=lambda_init),
        grid=(bsz, N_UNITS, nt),
        in_specs=[q_spec, k_spec, vt_spec, z_spec] + extra_specs,
        out_specs=pl.BlockSpec((None, tq, UNIT), lambda b, u, i: (b, i, u)),
        out_shape=jax.ShapeDtypeStruct((bsz, lp, N_UNITS * UNIT), BF16),
        scratch_shapes=[pltpu.VMEM((2, 1, tq), F32), pltpu.VMEM((2, 1, tq), F32),
                        pltpu.VMEM((2, UNIT, tq), F32)],
        compiler_params=pltpu.CompilerParams(
            dimension_semantics=("parallel", "parallel", "arbitrary"),
            vmem_limit_bytes=VMEM_LIMIT),
        name="fox_flash" if fox else "diff_flash",
    )(qt, k, vt, zg, *extra)


def _out_kernel(*refs, final):
    if final:
        ma_ref, mb_ref, w_ref, h_ref, g_ref, o_ref = refs
    else:
        ma_ref, mb_ref, w_ref, h_ref, o_ref = refs
    half = ma_ref.shape[-1]
    h = (h_ref[...]
         + jnp.dot(ma_ref[...], w_ref[0:half, :], preferred_element_type=F32)
         + jnp.dot(mb_ref[...], w_ref[half:2 * half, :], preferred_element_type=F32))
    if final:
        h = h * lax.rsqrt(jnp.mean(h * h, axis=-1, keepdims=True) + NORM_EPS) * g_ref[...]
    o_ref[...] = h


def _out_project(mixed_a, mixed_b, w_out, h, final_g, *, tl, rows):
    bsz, _, dm = h.shape
    half = mixed_a.shape[-1]
    final = final_g is not None
    row_spec = lambda width: pl.BlockSpec((None, tl, width), lambda b, i: (b, i, 0))
    const = lambda shape: pl.BlockSpec(shape, lambda b, i: (0,) * len(shape))
    in_specs = [row_spec(half), row_spec(half), const((2 * half, dm)), row_spec(dm)]
    args = [mixed_a, mixed_b, w_out, h]
    if final:
        in_specs.append(const((1, dm)))
        args.append(final_g)
    return pl.pallas_call(
        functools.partial(_out_kernel, final=final),
        grid=(bsz, rows // tl),
        in_specs=in_specs,
        out_specs=row_spec(dm),
        out_shape=jax.ShapeDtypeStruct((bsz, rows, dm), F32),
        compiler_params=pltpu.CompilerParams(
            dimension_semantics=("parallel", "parallel"), vmem_limit_bytes=VMEM_LIMIT),
        name="out_final" if final else "out_proj",
    )(*args)


def _rope_tables(seq, lp):
    half = DIFF_QK_DIM // 2
    pos = np.zeros((lp,), np.float64)
    pos[:seq] = N_META + np.arange(seq)
    pos[seq:seq + N_META] = np.arange(N_META)
    inv = ROPE_THETA ** (-np.arange(half, dtype=np.float64) / half)
    ang = pos[:, None] * np.tile(inv, UNIT // half)[None, :]
    sign = np.where(np.arange(UNIT) < UNIT // 2, -1.0, 1.0)[None, :]
    cos, sin = np.cos(ang), np.sin(ang) * sign
    q_scale = DIFF_QK_DIM ** -0.5 * LOG2E
    return tuple(jnp.asarray(t, F32) for t in (cos * q_scale, sin * q_scale, cos, sin))


def _unit_lane_order(w_qk):
    dm, width = w_qk.shape
    half = DIFF_QK_DIM // 2
    w5 = w_qk.reshape(dm, width // (4 * half), 2, 2, half)
    return w5.transpose(0, 1, 3, 2, 4).reshape(dm, width)


def _projection_weight(w_layer):
    width = N_UNITS * UNIT
    sec = lambda k: w_layer[:, k * width:(k + 1) * width]
    logits = jnp.pad(w_layer[:, 8 * width:], ((0, 0), (0, UNIT - FOX_HEADS)))
    return jnp.concatenate(
        [_unit_lane_order(sec(0)), _unit_lane_order(sec(1)), sec(2), sec(4), sec(5), sec(6),
         sec(3), sec(7), logits], axis=1).astype(BF16)


def kernel(x, meta_tokens, norm_g, w_in, b_forget, lam_q1, lam_k1, lam_q2, lam_k2, subln_g,
           w_out, final_g):
    bsz, seq, dm = x.shape
    depth = w_in.shape[0]
    tl = SEQ_TILE
    lp = seq + tl
    width = N_UNITS * UNIT
    assert seq % tl == 0 and N_META <= UNIT and w_in.shape[2] == 8 * width + FOX_HEADS

    meta = jnp.broadcast_to(meta_tokens.astype(x.dtype)[None], (bsz, N_META, dm))
    h = jnp.concatenate([x, meta, jnp.zeros((bsz, lp - seq - N_META, dm), x.dtype)], axis=1)

    tabs = _rope_tables(seq, lp)
    tri = jnp.asarray(np.tril(np.ones((tl, tl), np.float32)), BF16)

    out = None
    for layer in range(depth):
        lambda_init = 0.8 - 0.6 * math.exp(-0.3 * layer)
        w = _projection_weight(w_in[layer])
        bf = jnp.pad(b_forget[layer].astype(F32), (0, UNIT - FOX_HEADS)).reshape(1, UNIT)
        (qdt, kd, vdt, qft, kf, vft, zg, cqt, ckrep) = _project(
            h, norm_g[layer].reshape(1, dm), w, bf, tabs, tri, tl=tl)

        lam = jnp.stack([lam_q1[layer], lam_k1[layer], lam_q2[layer], lam_k2[layer]]).astype(F32)
        mixed_a = _flash(qdt, kd, vdt, zg, (lam, subln_g[layer].reshape(N_UNITS, 1, UNIT)),
                         fox=False, tq=tl, seq=seq, lambda_init=lambda_init)
        mixed_b = _flash(qft, kf, vft, zg,
                         (cqt.reshape(bsz, N_UNITS, 2, lp), ckrep.reshape(bsz, N_UNITS, 2, lp, UNIT)),
                         fox=True, tq=tl, seq=seq, lambda_init=lambda_init)

        w_o = w_out[layer].astype(BF16)
        if layer + 1 < depth:
            h = _out_project(mixed_a, mixed_b, w_o, h, None, tl=tl, rows=lp)
        else:
            out = _out_project(mixed_a, mixed_b, w_o, h, final_g.reshape(1, dm), tl=tl, rows=seq)
    return out
```

```python
import functools
import math

import numpy as np
import jax
import jax.numpy as jnp
from jax import lax
from jax.experimental import pallas as pl
from jax.experimental.pallas import tpu as pltpu

N_META = 16
DIFF_QK_DIM = 64
FOX_HEAD_DIM = 64
FOX_HEADS = 8
UNIT = 128
N_UNITS = 4
ROPE_THETA = 10000.0
NORM_EPS = 1e-6
LOG2E = 1.0 / math.log(2.0)
NEG = -1e30

SEQ_TILE = 512
VMEM_LIMIT = 56 * 1024 * 1024

F32 = jnp.float32
BF16 = jnp.bfloat16


def _proj_kernel(h_ref, g_ref, w_ref, bf_ref, cosq_ref, sinq_ref, cosk_ref, sink_ref, tri_ref,
                 qdt_ref, kd_ref, vdt_ref, qft_ref, kf_ref, vft_ref, zg_ref, cqt_ref, ckrep_ref,
                 carry_ref, *, tl, fox_scale):
    step = pl.program_id(1)

    h = h_ref[...]
    u = h * lax.rsqrt(jnp.mean(h * h, axis=-1, keepdims=True) + NORM_EPS) * g_ref[...]
    ub = u.astype(BF16)

    def proj(lo, width):
        return jnp.dot(ub, w_ref[:, lo:lo + width], preferred_element_type=F32)

    def rope(t, cos_ref, sin_ref):
        return t * cos_ref[...] + pltpu.roll(t, 64, 1) * sin_ref[...]

    width = N_UNITS * UNIT
    dq = proj(0 * width, width)
    for un in range(N_UNITS):
        qdt_ref[un] = rope(dq[:, un * UNIT:(un + 1) * UNIT], cosq_ref, sinq_ref).T.astype(BF16)
    dk = proj(1 * width, width)
    for un in range(N_UNITS):
        kd_ref[un] = rope(dk[:, un * UNIT:(un + 1) * UNIT], cosk_ref, sink_ref).astype(BF16)
    dv = proj(2 * width, width)
    for un in range(N_UNITS):
        vdt_ref[un] = dv[:, un * UNIT:(un + 1) * UNIT].T.astype(BF16)
    fq = proj(3 * width, width) * fox_scale
    for un in range(N_UNITS):
        qft_ref[un] = fq[:, un * UNIT:(un + 1) * UNIT].T.astype(BF16)
    fk = proj(4 * width, width)
    for un in range(N_UNITS):
        kf_ref[un] = fk[:, un * UNIT:(un + 1) * UNIT].astype(BF16)
    fv = proj(5 * width, width)
    for un in range(N_UNITS):
        vft_ref[un] = fv[:, un * UNIT:(un + 1) * UNIT].T.astype(BF16)

    z = proj(6 * width, 2 * width)
    zg_ref[...] = z * (1.0 / (1.0 + jnp.exp(-z)))

    fl = proj(8 * width, UNIT) + bf_ref[...]
    ls = (jnp.minimum(fl, 0.0) - jnp.log1p(jnp.exp(-jnp.abs(fl)))) * LOG2E
    row = lax.broadcasted_iota(jnp.int32, (tl, UNIT), 0)
    ls = jnp.where(row >= jnp.where(step == 0, N_META, tl), 0.0, ls)

    @pl.when(step == 0)
    def _():
        carry_ref[...] = jnp.zeros_like(carry_ref)

    hi = ls.astype(BF16)
    r1 = ls - hi.astype(F32)
    mid = r1.astype(BF16)
    lo = (r1 - mid.astype(F32)).astype(BF16)
    tri = tri_ref[...]
    cum = (jnp.dot(tri, hi, preferred_element_type=F32)
           + jnp.dot(tri, mid, preferred_element_type=F32)
           + jnp.dot(tri, lo, preferred_element_type=F32)) + carry_ref[...]
    carry_ref[...] = cum[tl - 1:tl, :]

    cqt_ref[...] = cum.T[0:FOX_HEADS, :]
    for hh in range(FOX_HEADS):
        ckrep_ref[hh] = jnp.broadcast_to(cum[:, hh:hh + 1], (tl, UNIT))


def _project(h, norm_g, w, b_forget, tabs, tri, *, tl):
    bsz, lp, dm = h.shape
    nl = lp // tl
    pw = w.shape[1]

    def tok(b, i):
        return (i + nl - 1) % nl

    row_spec = lambda width: pl.BlockSpec((None, tl, width), lambda b, i: (b, tok(b, i), 0))
    unit_rows = pl.BlockSpec((None, N_UNITS, tl, UNIT), lambda b, i: (b, 0, tok(b, i), 0))
    unit_cols = pl.BlockSpec((None, N_UNITS, UNIT, tl), lambda b, i: (b, 0, 0, tok(b, i)))
    tab_spec = pl.BlockSpec((tl, UNIT), lambda b, i: (tok(b, i), 0))
    const = lambda shape: pl.BlockSpec(shape, lambda b, i: (0,) * len(shape))

    unit_t = jax.ShapeDtypeStruct((bsz, N_UNITS, UNIT, lp), BF16)
    unit_n = jax.ShapeDtypeStruct((bsz, N_UNITS, lp, UNIT), BF16)
    return pl.pallas_call(
        functools.partial(_proj_kernel, tl=tl, fox_scale=FOX_HEAD_DIM ** -0.5 * LOG2E),
        grid=(bsz, nl),
        in_specs=[row_spec(dm), const((1, dm)), const((dm, pw)), const((1, UNIT)),
                  tab_spec, tab_spec, tab_spec, tab_spec, const((tl, tl))],
        out_specs=[unit_cols, unit_rows, unit_cols, unit_cols, unit_rows, unit_cols,
                   row_spec(2 * N_UNITS * UNIT),
                   pl.BlockSpec((None, FOX_HEADS, tl), lambda b, i: (b, 0, tok(b, i))),
                   pl.BlockSpec((None, FOX_HEADS, tl, UNIT), lambda b, i: (b, 0, tok(b, i), 0))],
        out_shape=[unit_t, unit_n, unit_t, unit_t, unit_n, unit_t,
                   jax.ShapeDtypeStruct((bsz, lp, 2 * N_UNITS * UNIT), F32),
                   jax.ShapeDtypeStruct((bsz, FOX_HEADS, lp), F32),
                   jax.ShapeDtypeStruct((bsz, FOX_HEADS, lp, UNIT), F32)],
        scratch_shapes=[pltpu.VMEM((1, UNIT), F32)],
        compiler_params=pltpu.CompilerParams(
            dimension_semantics=("arbitrary", "arbitrary"), vmem_limit_bytes=VMEM_LIMIT),
        name="proj",
    )(h, norm_g, w, b_forget, *tabs, tri)


def _flash_kernel(*refs, fox, tq, seq, lambda_init):
    if fox:
        qt_ref, k_ref, vt_ref, z_ref, cq_ref, ck_ref, o_ref = refs[:7]
    else:
        qt_ref, k_ref, vt_ref, z_ref, lam_ref, g_ref, o_ref = refs[:7]
    m_sc, l_sc, acc_sc, q_sc, s_even, s_odd = refs[7:]
    tk = tq
    i = pl.program_id(2)
    nq = pl.num_programs(2) - 1

    feat = lax.broadcasted_iota(jnp.int32, (UNIT, tq), 0)
    second = (feat >= FOX_HEAD_DIM) if fox else ((feat // 32) % 2 == 1)
    qt = qt_ref[...].astype(F32)
    q_sc[0] = jnp.where(second, 0.0, qt).astype(BF16)
    q_sc[1] = jnp.where(second, qt, 0.0).astype(BF16)

    m_sc[...] = jnp.full_like(m_sc, NEG)
    l_sc[...] = jnp.zeros_like(l_sc)
    acc_sc[...] = jnp.zeros_like(acc_sc)

    def scores(k_off, k_len, c):
        return jnp.dot(k_ref[pl.ds(k_off, k_len), :], q_sc[c], preferred_element_type=F32)

    def issue_scores(tile, dst):
        k_off = pl.multiple_of(tile * tk, tk)
        for c in range(2):
            dst[c] = scores(k_off, tk, c)

    def step(k_off, k_len, mask, src=None):
        vtt = vt_ref[:, pl.ds(k_off, k_len)]
        for c in range(2):
            s = scores(k_off, k_len, c) if src is None else src[c]
            if fox:
                ck = ck_ref[c, pl.ds(k_off, k_len), :]
                s = jnp.concatenate(
                    [s[:, b * UNIT:(b + 1) * UNIT] - ck for b in range(tq // UNIT)], axis=1)
            if mask is not None:
                s = jnp.where(mask, s, NEG)
            m_old = m_sc[c]
            m_blk = jnp.max(s, axis=0, keepdims=True)
            if fox:
                cq = cq_ref[c:c + 1, :]
                m_new = jnp.maximum(m_old, m_blk + cq)
                shift = m_new - cq
            else:
                m_new = jnp.maximum(m_old, m_blk)
                shift = m_new
            alpha = jnp.exp2(m_old - m_new)
            p = jnp.exp2(s - shift)
            l_sc[c] = alpha * l_sc[c] + jnp.sum(p, axis=0, keepdims=True)
            acc_sc[c] = alpha * acc_sc[c] + jnp.dot(vtt, p.astype(BF16),
                                                    preferred_element_type=F32)
            m_sc[c] = m_new

    @pl.when(i < nq)
    def _():
        key = lax.broadcasted_iota(jnp.int32, (UNIT, tq), 0)
        step(seq, UNIT, key < N_META)

        def full_step(tile, src):
            step(pl.multiple_of(tile * tk, tk), tk, None, src)

        def diag_step(src):
            key = lax.broadcasted_iota(jnp.int32, (tk, tq), 0)
            qry = lax.broadcasted_iota(jnp.int32, (tk, tq), 1)
            step(pl.multiple_of(i * tk, tk), tk, key <= qry, src)

        issue_scores(0, s_even)

        def body(pair, carry):
            issue_scores(2 * pair + 1, s_odd)
            full_step(2 * pair, s_even)
            issue_scores(2 * pair + 2, s_even)
            full_step(2 * pair + 1, s_odd)
            return carry

        lax.fori_loop(0, i // 2, body, 0)

        @pl.when(i % 2 == 0)
        def _():
            diag_step(s_even)

        @pl.when(i % 2 == 1)
        def _():
            issue_scores(i, s_odd)
            full_step(i - 1, s_even)
            diag_step(s_odd)

    @pl.when(i == nq)
    def _():
        key = lax.broadcasted_iota(jnp.int32, (UNIT, tq), 0)
        qry = lax.broadcasted_iota(jnp.int32, (UNIT, tq), 1)
        step(seq, UNIT, jnp.logical_and(key <= qry, key < N_META))

    o1 = acc_sc[0] * (1.0 / l_sc[0])
    o2 = acc_sc[1] * (1.0 / l_sc[1])
    if fox:
        o = jnp.concatenate([o1[:FOX_HEAD_DIM], o2[FOX_HEAD_DIM:]], axis=0).T
    else:
        lam = (jnp.exp(jnp.sum(lam_ref[0:1, :] * lam_ref[1:2, :], axis=-1, keepdims=True))
               - jnp.exp(jnp.sum(lam_ref[2:3, :] * lam_ref[3:4, :], axis=-1, keepdims=True))
               + lambda_init)
        o = (o1 - lam * o2).T
        o = o * lax.rsqrt(jnp.mean(o * o, axis=-1, keepdims=True) + NORM_EPS) * g_ref[...]
        o = o * (1.0 - lambda_init)
    o_ref[...] = (o * z_ref[...]).astype(BF16)


def _flash(qt, k, vt, zg, extra, *, fox, tq, seq, lambda_init):
    bsz, _, lp, _ = k.shape
    nt = lp // tq
    z_off = N_UNITS if fox else 0
    q_spec = pl.BlockSpec((None, None, UNIT, tq), lambda b, u, i: (b, u, 0, i))
    k_spec = pl.BlockSpec((None, None, lp, UNIT), lambda b, u, i: (b, u, 0, 0))
    vt_spec = pl.BlockSpec((None, None, UNIT, lp), lambda b, u, i: (b, u, 0, 0))
    z_spec = pl.BlockSpec((None, tq, UNIT), lambda b, u, i: (b, i, u + z_off))
    if fox:
        extra_specs = [pl.BlockSpec((None, None, 2, tq), lambda b, u, i: (b, u, 0, i)),
                       pl.BlockSpec((None, None, 2, lp, UNIT), lambda b, u, i: (b, u, 0, 0, 0))]
    else:
        extra_specs = [pl.BlockSpec((4, DIFF_QK_DIM), lambda b, u, i: (0, 0)),
                       pl.BlockSpec((None, 1, UNIT), lambda b, u, i: (u, 0, 0))]
    return pl.pallas_call(
        functools.partial(_flash_kernel, fox=fox, tq=tq, seq=seq, lambda_init=lambda_init),
        grid=(bsz, N_UNITS, nt),
        in_specs=[q_spec, k_spec, vt_spec, z_spec] + extra_specs,
        out_specs=pl.BlockSpec((None, tq, UNIT), lambda b, u, i: (b, i, u)),
        out_shape=jax.ShapeDtypeStruct((bsz, lp, N_UNITS * UNIT), BF16),
        scratch_shapes=[pltpu.VMEM((2, 1, tq), F32), pltpu.VMEM((2, 1, tq), F32),
                        pltpu.VMEM((2, UNIT, tq), F32), pltpu.VMEM((2, UNIT, tq), BF16),
                        pltpu.VMEM((2, tq, tq), F32), pltpu.VMEM((2, tq, tq), F32)],
        compiler_params=pltpu.CompilerParams(
            dimension_semantics=("parallel", "parallel", "arbitrary"),
            vmem_limit_bytes=VMEM_LIMIT),
        name="fox_flash" if fox else "diff_flash",
    )(qt, k, vt, zg, *extra)


def _out_kernel(*refs, final):
    if final:
        ma_ref, mb_ref, w_ref, h_ref, g_ref, o_ref = refs
    else:
        ma_ref, mb_ref, w_ref, h_ref, o_ref = refs
    half = ma_ref.shape[-1]
    h = (h_ref[...]
         + jnp.dot(ma_ref[...], w_ref[0:half, :], preferred_element_type=F32)
         + jnp.dot(mb_ref[...], w_ref[half:2 * half, :], preferred_element_type=F32))
    if final:
        h = h * lax.rsqrt(jnp.mean(h * h, axis=-1, keepdims=True) + NORM_EPS) * g_ref[...]
    o_ref[...] = h


def _out_project(mixed_a, mixed_b, w_out, h, final_g, *, tl, rows):
    bsz, _, dm = h.shape
    half = mixed_a.shape[-1]
    final = final_g is not None
    row_spec = lambda width: pl.BlockSpec((None, tl, width), lambda b, i: (b, i, 0))
    const = lambda shape: pl.BlockSpec(shape, lambda b, i: (0,) * len(shape))
    in_specs = [row_spec(half), row_spec(half), const((2 * half, dm)), row_spec(dm)]
    args = [mixed_a, mixed_b, w_out, h]
    if final:
        in_specs.append(const((1, dm)))
        args.append(final_g)
    return pl.pallas_call(
        functools.partial(_out_kernel, final=final),
        grid=(bsz, rows // tl),
        in_specs=in_specs,
        out_specs=row_spec(dm),
        out_shape=jax.ShapeDtypeStruct((bsz, rows, dm), F32),
        compiler_params=pltpu.CompilerParams(
            dimension_semantics=("parallel", "parallel"), vmem_limit_bytes=VMEM_LIMIT),
        name="out_final" if final else "out_proj",
    )(*args)


def _rope_tables(seq, lp):
    half = DIFF_QK_DIM // 2
    pos = np.zeros((lp,), np.float64)
    pos[:seq] = N_META + np.arange(seq)
    pos[seq:seq + N_META] = np.arange(N_META)
    inv = ROPE_THETA ** (-np.arange(half, dtype=np.float64) / half)
    ang = pos[:, None] * np.tile(inv, UNIT // half)[None, :]
    sign = np.where(np.arange(UNIT) < UNIT // 2, -1.0, 1.0)[None, :]
    cos, sin = np.cos(ang), np.sin(ang) * sign
    q_scale = DIFF_QK_DIM ** -0.5 * LOG2E
    return tuple(jnp.asarray(t, F32) for t in (cos * q_scale, sin * q_scale, cos, sin))


def _unit_lane_order(w_qk):
    dm, width = w_qk.shape
    half = DIFF_QK_DIM // 2
    w5 = w_qk.reshape(dm, width // (4 * half), 2, 2, half)
    return w5.transpose(0, 1, 3, 2, 4).reshape(dm, width)


def _projection_weight(w_layer):
    width = N_UNITS * UNIT
    sec = lambda k: w_layer[:, k * width:(k + 1) * width]
    logits = jnp.pad(w_layer[:, 8 * width:], ((0, 0), (0, UNIT - FOX_HEADS)))
    return jnp.concatenate(
        [_unit_lane_order(sec(0)), _unit_lane_order(sec(1)), sec(2), sec(4), sec(5), sec(6),
         sec(3), sec(7), logits], axis=1).astype(BF16)


def kernel(x, meta_tokens, norm_g, w_in, b_forget, lam_q1, lam_k1, lam_q2, lam_k2, subln_g,
           w_out, final_g):
    bsz, seq, dm = x.shape
    depth = w_in.shape[0]
    tl = SEQ_TILE
    lp = seq + tl
    width = N_UNITS * UNIT
    assert seq % tl == 0 and N_META <= UNIT and w_in.shape[2] == 8 * width + FOX_HEADS

    meta = jnp.broadcast_to(meta_tokens.astype(x.dtype)[None], (bsz, N_META, dm))
    h = jnp.concatenate([x, meta, jnp.zeros((bsz, lp - seq - N_META, dm), x.dtype)], axis=1)

    tabs = _rope_tables(seq, lp)
    tri = jnp.asarray(np.tril(np.ones((tl, tl), np.float32)), BF16)

    out = None
    for layer in range(depth):
        lambda_init = 0.8 - 0.6 * math.exp(-0.3 * layer)
        w = _projection_weight(w_in[layer])
        bf = jnp.pad(b_forget[layer].astype(F32), (0, UNIT - FOX_HEADS)).reshape(1, UNIT)
        (qdt, kd, vdt, qft, kf, vft, zg, cqt, ckrep) = _project(
            h, norm_g[layer].reshape(1, dm), w, bf, tabs, tri, tl=tl)

        lam = jnp.stack([lam_q1[layer], lam_k1[layer], lam_q2[layer], lam_k2[layer]]).astype(F32)
        mixed_a = _flash(qdt, kd, vdt, zg, (lam, subln_g[layer].reshape(N_UNITS, 1, UNIT)),
                         fox=False, tq=tl, seq=seq, lambda_init=lambda_init)
        mixed_b = _flash(qft, kf, vft, zg,
                         (cqt.reshape(bsz, N_UNITS, 2, lp), ckrep.reshape(bsz, N_UNITS, 2, lp, UNIT)),
                         fox=True, tq=tl, seq=seq, lambda_init=lambda_init)

        w_o = w_out[layer].astype(BF16)
        if layer + 1 < depth:
            h = _out_project(mixed_a, mixed_b, w_o, h, None, tl=tl, rows=lp)
        else:
            out = _out_project(mixed_a, mixed_b, w_o, h, final_g.reshape(1, dm), tl=tl, rows=seq)
    return out
```

```python
import functools
import math

import numpy as np
import jax
import jax.numpy as jnp
from jax import lax
from jax.experimental import pallas as pl
from jax.experimental.pallas import tpu as pltpu

N_META = 16
DIFF_QK_DIM = 64
FOX_HEAD_DIM = 64
FOX_HEADS = 8
UNIT = 128
N_UNITS = 4
ROPE_THETA = 10000.0
NORM_EPS = 1e-6
LOG2E = 1.0 / math.log(2.0)
NEG = -1e30

SEQ_TILE = 512
VMEM_LIMIT = 56 * 1024 * 1024

F32 = jnp.float32
BF16 = jnp.bfloat16


def _proj_kernel(h_ref, g_ref, w_ref, bf_ref, cosq_ref, sinq_ref, cosk_ref, sink_ref, tri_ref,
                 qdt_ref, kd_ref, vdt_ref, qft_ref, kf_ref, vft_ref, zg_ref, cqt_ref, ckrep_ref,
                 carry_ref, *, tl, fox_scale):
    step = pl.program_id(1)

    h = h_ref[...]
    u = h * lax.rsqrt(jnp.mean(h * h, axis=-1, keepdims=True) + NORM_EPS) * g_ref[...]
    ub = u.astype(BF16)

    def proj(lo, width):
        return jnp.dot(ub, w_ref[:, lo:lo + width], preferred_element_type=F32)

    def rope(t, cos_ref, sin_ref):
        return t * cos_ref[...] + pltpu.roll(t, 64, 1) * sin_ref[...]

    width = N_UNITS * UNIT
    dq = proj(0 * width, width)
    for un in range(N_UNITS):
        qdt_ref[un] = rope(dq[:, un * UNIT:(un + 1) * UNIT], cosq_ref, sinq_ref).T.astype(BF16)
    dk = proj(1 * width, width)
    for un in range(N_UNITS):
        kd_ref[un] = rope(dk[:, un * UNIT:(un + 1) * UNIT], cosk_ref, sink_ref).astype(BF16)
    dv = proj(2 * width, width)
    for un in range(N_UNITS):
        vdt_ref[un] = dv[:, un * UNIT:(un + 1) * UNIT].T.astype(BF16)
    fq = proj(3 * width, width) * fox_scale
    for un in range(N_UNITS):
        qft_ref[un] = fq[:, un * UNIT:(un + 1) * UNIT].T.astype(BF16)
    fk = proj(4 * width, width)
    for un in range(N_UNITS):
        kf_ref[un] = fk[:, un * UNIT:(un + 1) * UNIT].astype(BF16)
    fv = proj(5 * width, width)
    for un in range(N_UNITS):
        vft_ref[un] = fv[:, un * UNIT:(un + 1) * UNIT].T.astype(BF16)

    z = proj(6 * width, 2 * width)
    zg_ref[...] = z * (1.0 / (1.0 + jnp.exp(-z)))

    fl = proj(8 * width, UNIT) + bf_ref[...]
    ls = (jnp.minimum(fl, 0.0) - jnp.log1p(jnp.exp(-jnp.abs(fl)))) * LOG2E
    row = lax.broadcasted_iota(jnp.int32, (tl, UNIT), 0)
    ls = jnp.where(row >= jnp.where(step == 0, N_META, tl), 0.0, ls)

    @pl.when(step == 0)
    def _():
        carry_ref[...] = jnp.zeros_like(carry_ref)

    hi = ls.astype(BF16)
    r1 = ls - hi.astype(F32)
    mid = r1.astype(BF16)
    lo = (r1 - mid.astype(F32)).astype(BF16)
    tri = tri_ref[...]
    cum = (jnp.dot(tri, hi, preferred_element_type=F32)
           + jnp.dot(tri, mid, preferred_element_type=F32)
           + jnp.dot(tri, lo, preferred_element_type=F32)) + carry_ref[...]
    carry_ref[...] = cum[tl - 1:tl, :]

    cqt_ref[...] = cum.T[0:FOX_HEADS, :]
    for hh in range(FOX_HEADS):
        ckrep_ref[hh] = jnp.broadcast_to(cum[:, hh:hh + 1], (tl, UNIT))


def _project(h, norm_g, w, b_forget, tabs, tri, *, tl):
    bsz, lp, dm = h.shape
    nl = lp // tl
    pw = w.shape[1]

    def tok(b, i):
        return (i + nl - 1) % nl

    row_spec = lambda width: pl.BlockSpec((None, tl, width), lambda b, i: (b, tok(b, i), 0))
    unit_rows = pl.BlockSpec((None, N_UNITS, tl, UNIT), lambda b, i: (b, 0, tok(b, i), 0))
    unit_cols = pl.BlockSpec((None, N_UNITS, UNIT, tl), lambda b, i: (b, 0, 0, tok(b, i)))
    tab_spec = pl.BlockSpec((tl, UNIT), lambda b, i: (tok(b, i), 0))
    const = lambda shape: pl.BlockSpec(shape, lambda b, i: (0,) * len(shape))

    unit_t = jax.ShapeDtypeStruct((bsz, N_UNITS, UNIT, lp), BF16)
    unit_n = jax.ShapeDtypeStruct((bsz, N_UNITS, lp, UNIT), BF16)
    return pl.pallas_call(
        functools.partial(_proj_kernel, tl=tl, fox_scale=FOX_HEAD_DIM ** -0.5 * LOG2E),
        grid=(bsz, nl),
        in_specs=[row_spec(dm), const((1, dm)), const((dm, pw)), const((1, UNIT)),
                  tab_spec, tab_spec, tab_spec, tab_spec, const((tl, tl))],
        out_specs=[unit_cols, unit_rows, unit_cols, unit_cols, unit_rows, unit_cols,
                   row_spec(2 * N_UNITS * UNIT),
                   pl.BlockSpec((None, FOX_HEADS, tl), lambda b, i: (b, 0, tok(b, i))),
                   pl.BlockSpec((None, FOX_HEADS, tl, UNIT), lambda b, i: (b, 0, tok(b, i), 0))],
        out_shape=[unit_t, unit_n, unit_t, unit_t, unit_n, unit_t,
                   jax.ShapeDtypeStruct((bsz, lp, 2 * N_UNITS * UNIT), F32),
                   jax.ShapeDtypeStruct((bsz, FOX_HEADS, lp), F32),
                   jax.ShapeDtypeStruct((bsz, FOX_HEADS, lp, UNIT), F32)],
        scratch_shapes=[pltpu.VMEM((1, UNIT), F32)],
        compiler_params=pltpu.CompilerParams(
            dimension_semantics=("arbitrary", "arbitrary"), vmem_limit_bytes=VMEM_LIMIT),
        name="proj",
    )(h, norm_g, w, b_forget, *tabs, tri)


def _flash_kernel(*refs, fox, tq, seq, lambda_init):
    if fox:
        qt_ref, k_ref, vt_ref, z_ref, cq_ref, ck_ref, o_ref = refs[:7]
    else:
        qt_ref, k_ref, vt_ref, z_ref, lam_ref, g_ref, o_ref = refs[:7]
    m_sc, l_sc, acc_sc, q_sc, s_even, s_odd = refs[7:]
    tk = tq
    nq = seq // tq
    n_steps = nq * (nq + 1) // 2

    def tile_start(t):
        return pl.multiple_of(t * tq, tq)

    def load_queries(i):
        feat = lax.broadcasted_iota(jnp.int32, (UNIT, tq), 0)
        second = (feat >= FOX_HEAD_DIM) if fox else ((feat // 32) % 2 == 1)
        qt = qt_ref[:, pl.ds(tile_start(i), tq)].astype(F32)
        q_sc[0] = jnp.where(second, 0.0, qt).astype(BF16)
        q_sc[1] = jnp.where(second, qt, 0.0).astype(BF16)

    def reset():
        m_sc[...] = jnp.full_like(m_sc, NEG)
        l_sc[...] = jnp.zeros_like(l_sc)
        acc_sc[...] = jnp.zeros_like(acc_sc)

    def scores(k_off, k_len, c):
        return jnp.dot(k_ref[pl.ds(k_off, k_len), :], q_sc[c], preferred_element_type=F32)

    def issue_scores(j, dst):
        for c in range(2):
            dst[c] = scores(tile_start(j), tk, c)

    def softmax_pv(i, k_off, k_len, mask, src):
        vtt = vt_ref[:, pl.ds(k_off, k_len)]
        for c in range(2):
            s = src[c]
            if fox:
                ck = ck_ref[c, pl.ds(k_off, k_len), :]
                s = jnp.concatenate(
                    [s[:, b * UNIT:(b + 1) * UNIT] - ck for b in range(tq // UNIT)], axis=1)
            if mask is not None:
                s = jnp.where(mask, s, NEG)
            m_old = m_sc[c]
            m_blk = jnp.max(s, axis=0, keepdims=True)
            if fox:
                cq = cq_ref[c:c + 1, pl.ds(tile_start(i), tq)]
                m_new = jnp.maximum(m_old, m_blk + cq)
                shift = m_new - cq
            else:
                m_new = jnp.maximum(m_old, m_blk)
                shift = m_new
            alpha = jnp.exp2(m_old - m_new)
            p = jnp.exp2(s - shift)
            l_sc[c] = alpha * l_sc[c] + jnp.sum(p, axis=0, keepdims=True)
            acc_sc[c] = alpha * acc_sc[c] + jnp.dot(vtt, p.astype(BF16),
                                                    preferred_element_type=F32)
            m_sc[c] = m_new

    def meta_keys_step(i, causal):
        key = lax.broadcasted_iota(jnp.int32, (UNIT, tq), 0)
        mask = key < N_META
        if causal:
            mask = jnp.logical_and(mask, key <= lax.broadcasted_iota(jnp.int32, (UNIT, tq), 1))
        softmax_pv(i, seq, UNIT, mask, [scores(seq, UNIT, c) for c in range(2)])

    def finalize(i):
        o1 = acc_sc[0] * (1.0 / l_sc[0])
        o2 = acc_sc[1] * (1.0 / l_sc[1])
        if fox:
            o = jnp.concatenate([o1[:FOX_HEAD_DIM], o2[FOX_HEAD_DIM:]], axis=0).T
        else:
            lam = (jnp.exp(jnp.sum(lam_ref[0:1, :] * lam_ref[1:2, :], axis=-1, keepdims=True))
                   - jnp.exp(jnp.sum(lam_ref[2:3, :] * lam_ref[3:4, :], axis=-1, keepdims=True))
                   + lambda_init)
            o = (o1 - lam * o2).T
            o = o * lax.rsqrt(jnp.mean(o * o, axis=-1, keepdims=True) + NORM_EPS) * g_ref[...]
            o = o * (1.0 - lambda_init)
        rows = pl.ds(tile_start(i), tq)
        o_ref[rows, :] = (o * z_ref[rows, :]).astype(BF16)

    def step(i, j, src, dst):
        @pl.when(j < i)
        def _():
            issue_scores(j + 1, dst)
            softmax_pv(i, tile_start(j), tk, None, src)

        @pl.when(j == i)
        def _():
            nxt = jnp.minimum(i + 1, nq - 1)
            load_queries(nxt)
            issue_scores(0, dst)
            key = lax.broadcasted_iota(jnp.int32, (tk, tq), 0)
            qry = lax.broadcasted_iota(jnp.int32, (tk, tq), 1)
            softmax_pv(i, tile_start(j), tk, key <= qry, src)
            finalize(i)
            reset()
            meta_keys_step(nxt, causal=False)

        wrap = j == i
        return jnp.where(wrap, i + 1, i), jnp.where(wrap, 0, j + 1)

    reset()
    load_queries(0)
    issue_scores(0, s_even)
    meta_keys_step(0, causal=False)

    def body(_, ij):
        i, j = step(ij[0], ij[1], s_even, s_odd)
        return step(i, j, s_odd, s_even)

    lax.fori_loop(0, n_steps // 2, body, (jnp.int32(0), jnp.int32(0)))

    reset()
    load_queries(nq)
    meta_keys_step(nq, causal=True)
    finalize(nq)


def _flash(qt, k, vt, zg, extra, *, fox, tq, seq, lambda_init):
    bsz, _, lp, _ = k.shape
    nq = seq // tq
    assert lp == seq + tq and (nq * (nq + 1) // 2) % 2 == 0
    z_off = N_UNITS if fox else 0
    q_spec = pl.BlockSpec((None, None, UNIT, lp), lambda b, u: (b, u, 0, 0))
    k_spec = pl.BlockSpec((None, None, lp, UNIT), lambda b, u: (b, u, 0, 0))
    z_spec = pl.BlockSpec((None, lp, UNIT), lambda b, u: (b, 0, u + z_off))
    if fox:
        extra_specs = [pl.BlockSpec((None, None, 2, lp), lambda b, u: (b, u, 0, 0)),
                       pl.BlockSpec((None, None, 2, lp, UNIT), lambda b, u: (b, u, 0, 0, 0))]
    else:
        extra_specs = [pl.BlockSpec((4, DIFF_QK_DIM), lambda b, u: (0, 0)),
                       pl.BlockSpec((None, 1, UNIT), lambda b, u: (u, 0, 0))]
    return pl.pallas_call(
        functools.partial(_flash_kernel, fox=fox, tq=tq, seq=seq, lambda_init=lambda_init),
        grid=(bsz, N_UNITS),
        in_specs=[q_spec, k_spec, q_spec, z_spec] + extra_specs,
        out_specs=pl.BlockSpec((None, lp, UNIT), lambda b, u: (b, 0, u)),
        out_shape=jax.ShapeDtypeStruct((bsz, lp, N_UNITS * UNIT), BF16),
        scratch_shapes=[pltpu.VMEM((2, 1, tq), F32), pltpu.VMEM((2, 1, tq), F32),
                        pltpu.VMEM((2, UNIT, tq), F32), pltpu.VMEM((2, UNIT, tq), BF16),
                        pltpu.VMEM((2, tq, tq), F32), pltpu.VMEM((2, tq, tq), F32)],
        compiler_params=pltpu.CompilerParams(
            dimension_semantics=("parallel", "parallel"), vmem_limit_bytes=VMEM_LIMIT),
        name="fox_flash" if fox else "diff_flash",
    )(qt, k, vt, zg, *extra)


def _out_kernel(*refs, final):
    if final:
        ma_ref, mb_ref, w_ref, h_ref, g_ref, o_ref = refs
    else:
        ma_ref, mb_ref, w_ref, h_ref, o_ref = refs
    half = ma_ref.shape[-1]
    h = (h_ref[...]
         + jnp.dot(ma_ref[...], w_ref[0:half, :], preferred_element_type=F32)
         + jnp.dot(mb_ref[...], w_ref[half:2 * half, :], preferred_element_type=F32))
    if final:
        h = h * lax.rsqrt(jnp.mean(h * h, axis=-1, keepdims=True) + NORM_EPS) * g_ref[...]
    o_ref[...] = h


def _out_project(mixed_a, mixed_b, w_out, h, final_g, *, tl, rows):
    bsz, _, dm = h.shape
    half = mixed_a.shape[-1]
    final = final_g is not None
    row_spec = lambda width: pl.BlockSpec((None, tl, width), lambda b, i: (b, i, 0))
    const = lambda shape: pl.BlockSpec(shape, lambda b, i: (0,) * len(shape))
    in_specs = [row_spec(half), row_spec(half), const((2 * half, dm)), row_spec(dm)]
    args = [mixed_a, mixed_b, w_out, h]
    if final:
        in_specs.append(const((1, dm)))
        args.append(final_g)
    return pl.pallas_call(
        functools.partial(_out_kernel, final=final),
        grid=(bsz, rows // tl),
        in_specs=in_specs,
        out_specs=row_spec(dm),
        out_shape=jax.ShapeDtypeStruct((bsz, rows, dm), F32),
        compiler_params=pltpu.CompilerParams(
            dimension_semantics=("parallel", "parallel"), vmem_limit_bytes=VMEM_LIMIT),
        name="out_final" if final else "out_proj",
    )(*args)


def _rope_tables(seq, lp):
    half = DIFF_QK_DIM // 2
    pos = np.zeros((lp,), np.float64)
    pos[:seq] = N_META + np.arange(seq)
    pos[seq:seq + N_META] = np.arange(N_META)
    inv = ROPE_THETA ** (-np.arange(half, dtype=np.float64) / half)
    ang = pos[:, None] * np.tile(inv, UNIT // half)[None, :]
    sign = np.where(np.arange(UNIT) < UNIT // 2, -1.0, 1.0)[None, :]
    cos, sin = np.cos(ang), np.sin(ang) * sign
    q_scale = DIFF_QK_DIM ** -0.5 * LOG2E
    return tuple(jnp.asarray(t, F32) for t in (cos * q_scale, sin * q_scale, cos, sin))


def _unit_lane_order(w_qk):
    dm, width = w_qk.shape
    half = DIFF_QK_DIM // 2
    w5 = w_qk.reshape(dm, width // (4 * half), 2, 2, half)
    return w5.transpose(0, 1, 3, 2, 4).reshape(dm, width)


def _projection_weight(w_layer):
    width = N_UNITS * UNIT
    sec = lambda k: w_layer[:, k * width:(k + 1) * width]
    logits = jnp.pad(w_layer[:, 8 * width:], ((0, 0), (0, UNIT - FOX_HEADS)))
    return jnp.concatenate(
        [_unit_lane_order(sec(0)), _unit_lane_order(sec(1)), sec(2), sec(4), sec(5), sec(6),
         sec(3), sec(7), logits], axis=1).astype(BF16)


def kernel(x, meta_tokens, norm_g, w_in, b_forget, lam_q1, lam_k1, lam_q2, lam_k2, subln_g,
           w_out, final_g):
    bsz, seq, dm = x.shape
    depth = w_in.shape[0]
    tl = SEQ_TILE
    lp = seq + tl
    width = N_UNITS * UNIT
    assert seq % tl == 0 and N_META <= UNIT and w_in.shape[2] == 8 * width + FOX_HEADS

    meta = jnp.broadcast_to(meta_tokens.astype(x.dtype)[None], (bsz, N_META, dm))
    h = jnp.concatenate([x, meta, jnp.zeros((bsz, lp - seq - N_META, dm), x.dtype)], axis=1)

    tabs = _rope_tables(seq, lp)
    tri = jnp.asarray(np.tril(np.ones((tl, tl), np.float32)), BF16)

    out = None
    for layer in range(depth):
        lambda_init = 0.8 - 0.6 * math.exp(-0.3 * layer)
        w = _projection_weight(w_in[layer])
        bf = jnp.pad(b_forget[layer].astype(F32), (0, UNIT - FOX_HEADS)).reshape(1, UNIT)
        (qdt, kd, vdt, qft, kf, vft, zg, cqt, ckrep) = _project(
            h, norm_g[layer].reshape(1, dm), w, bf, tabs, tri, tl=tl)

        lam = jnp.stack([lam_q1[layer], lam_k1[layer], lam_q2[layer], lam_k2[layer]]).astype(F32)
        mixed_a = _flash(qdt, kd, vdt, zg, (lam, subln_g[layer].reshape(N_UNITS, 1, UNIT)),
                         fox=False, tq=tl, seq=seq, lambda_init=lambda_init)
        mixed_b = _flash(qft, kf, vft, zg,
                         (cqt.reshape(bsz, N_UNITS, 2, lp), ckrep.reshape(bsz, N_UNITS, 2, lp, UNIT)),
                         fox=True, tq=tl, seq=seq, lambda_init=lambda_init)

        w_o = w_out[layer].astype(BF16)
        if layer + 1 < depth:
            h = _out_project(mixed_a, mixed_b, w_o, h, None, tl=tl, rows=lp)
        else:
            out = _out_project(mixed_a, mixed_b, w_o, h, final_g.reshape(1, dm), tl=tl, rows=seq)
    return out
```

```python
import functools
import math

import numpy as np
import jax
import jax.numpy as jnp
from jax import lax
from jax.experimental import pallas as pl
from jax.experimental.pallas import tpu as pltpu

N_META = 16
DIFF_QK_DIM = 64
FOX_HEAD_DIM = 64
FOX_HEADS = 8
UNIT = 128
N_UNITS = 4
ROPE_THETA = 10000.0
NORM_EPS = 1e-6
LOG2E = 1.0 / math.log(2.0)
NEG = -1e30
ONES_ROWS = 16

SEQ_TILE = 512
VMEM_LIMIT = 56 * 1024 * 1024

F32 = jnp.float32
BF16 = jnp.bfloat16


def _proj_kernel(h_ref, g_ref, w_ref, bf_ref, cosq_ref, sinq_ref, cosk_ref, sink_ref, tri_ref,
                 qdt_ref, kd_ref, vdt_ref, qft_ref, kf_ref, vft_ref, zg_ref, cqt_ref, ckrep_ref,
                 carry_ref, *, tl, fox_scale):
    step = pl.program_id(1)

    h = h_ref[...]
    u = h * lax.rsqrt(jnp.mean(h * h, axis=-1, keepdims=True) + NORM_EPS) * g_ref[...]
    ub = u.astype(BF16)

    def proj(lo, width):
        return jnp.dot(ub, w_ref[:, lo:lo + width], preferred_element_type=F32)

    def rope(t, cos_ref, sin_ref):
        return t * cos_ref[...] + pltpu.roll(t, 64, 1) * sin_ref[...]

    width = N_UNITS * UNIT
    dq = proj(0 * width, width)
    for un in range(N_UNITS):
        qdt_ref[un] = rope(dq[:, un * UNIT:(un + 1) * UNIT], cosq_ref, sinq_ref).T.astype(BF16)
    dk = proj(1 * width, width)
    for un in range(N_UNITS):
        kd_ref[un] = rope(dk[:, un * UNIT:(un + 1) * UNIT], cosk_ref, sink_ref).astype(BF16)
    dv = proj(2 * width, width)
    for un in range(N_UNITS):
        vdt_ref[un] = dv[:, un * UNIT:(un + 1) * UNIT].T.astype(BF16)
    fq = proj(3 * width, width) * fox_scale
    for un in range(N_UNITS):
        qft_ref[un] = fq[:, un * UNIT:(un + 1) * UNIT].T.astype(BF16)
    fk = proj(4 * width, width)
    for un in range(N_UNITS):
        kf_ref[un] = fk[:, un * UNIT:(un + 1) * UNIT].astype(BF16)
    fv = proj(5 * width, width)
    for un in range(N_UNITS):
        vft_ref[un] = fv[:, un * UNIT:(un + 1) * UNIT].T.astype(BF16)

    z = proj(6 * width, 2 * width)
    zg_ref[...] = z * (1.0 / (1.0 + jnp.exp(-z)))

    fl = proj(8 * width, UNIT) + bf_ref[...]
    ls = (jnp.minimum(fl, 0.0) - jnp.log1p(jnp.exp(-jnp.abs(fl)))) * LOG2E
    row = lax.broadcasted_iota(jnp.int32, (tl, UNIT), 0)
    ls = jnp.where(row >= jnp.where(step == 0, N_META, tl), 0.0, ls)

    @pl.when(step == 0)
    def _():
        carry_ref[...] = jnp.zeros_like(carry_ref)

    hi = ls.astype(BF16)
    r1 = ls - hi.astype(F32)
    mid = r1.astype(BF16)
    lo = (r1 - mid.astype(F32)).astype(BF16)
    tri = tri_ref[...]
    cum = (jnp.dot(tri, hi, preferred_element_type=F32)
           + jnp.dot(tri, mid, preferred_element_type=F32)
           + jnp.dot(tri, lo, preferred_element_type=F32)) + carry_ref[...]
    carry_ref[...] = cum[tl - 1:tl, :]

    cqt_ref[...] = cum.T[0:FOX_HEADS, :]
    for hh in range(FOX_HEADS):
        ckrep_ref[hh] = jnp.broadcast_to(cum[:, hh:hh + 1], (tl, UNIT))


def _project(h, norm_g, w, b_forget, tabs, tri, *, tl):
    bsz, lp, dm = h.shape
    nl = lp // tl
    pw = w.shape[1]

    def tok(b, i):
        return (i + nl - 1) % nl

    row_spec = lambda width: pl.BlockSpec((None, tl, width), lambda b, i: (b, tok(b, i), 0))
    unit_rows = pl.BlockSpec((None, N_UNITS, tl, UNIT), lambda b, i: (b, 0, tok(b, i), 0))
    unit_cols = pl.BlockSpec((None, N_UNITS, UNIT, tl), lambda b, i: (b, 0, 0, tok(b, i)))
    tab_spec = pl.BlockSpec((tl, UNIT), lambda b, i: (tok(b, i), 0))
    const = lambda shape: pl.BlockSpec(shape, lambda b, i: (0,) * len(shape))

    unit_t = jax.ShapeDtypeStruct((bsz, N_UNITS, UNIT, lp), BF16)
    unit_n = jax.ShapeDtypeStruct((bsz, N_UNITS, lp, UNIT), BF16)
    return pl.pallas_call(
        functools.partial(_proj_kernel, tl=tl, fox_scale=FOX_HEAD_DIM ** -0.5 * LOG2E),
        grid=(bsz, nl),
        in_specs=[row_spec(dm), const((1, dm)), const((dm, pw)), const((1, UNIT)),
                  tab_spec, tab_spec, tab_spec, tab_spec, const((tl, tl))],
        out_specs=[unit_cols, unit_rows, unit_cols, unit_cols, unit_rows, unit_cols,
                   row_spec(2 * N_UNITS * UNIT),
                   pl.BlockSpec((None, FOX_HEADS, tl), lambda b, i: (b, 0, tok(b, i))),
                   pl.BlockSpec((None, FOX_HEADS, tl, UNIT), lambda b, i: (b, 0, tok(b, i), 0))],
        out_shape=[unit_t, unit_n, unit_t, unit_t, unit_n, unit_t,
                   jax.ShapeDtypeStruct((bsz, lp, 2 * N_UNITS * UNIT), F32),
                   jax.ShapeDtypeStruct((bsz, FOX_HEADS, lp), F32),
                   jax.ShapeDtypeStruct((bsz, FOX_HEADS, lp, UNIT), F32)],
        scratch_shapes=[pltpu.VMEM((1, UNIT), F32)],
        compiler_params=pltpu.CompilerParams(
            dimension_semantics=("arbitrary", "arbitrary"), vmem_limit_bytes=VMEM_LIMIT),
        name="proj",
    )(h, norm_g, w, b_forget, *tabs, tri)


def _flash_kernel(*refs, fox, tq, seq, lambda_init):
    if fox:
        qt_ref, k_ref, vt_ref, z_ref, cq_ref, ck_ref, o_ref = refs[:7]
    else:
        qt_ref, k_ref, vt_ref, z_ref, lam_ref, g_ref, o_ref = refs[:7]
    m_sc, acc_sc, acc0_sc, q_sc = refs[7:11]
    s_buf, p_buf, alpha_buf, mblk_buf = (refs[11 + 2 * n:13 + 2 * n] for n in range(4))
    tk = tq
    nq = seq // tq
    n_steps = nq * (nq + 1) // 2

    def tile_start(t):
        return pl.multiple_of(t * tq, tq)

    def load_queries(i):
        feat = lax.broadcasted_iota(jnp.int32, (UNIT, tq), 0)
        second = (feat >= FOX_HEAD_DIM) if fox else ((feat // 32) % 2 == 1)
        qt = qt_ref[:, pl.ds(tile_start(i), tq)].astype(F32)
        q_sc[0] = jnp.where(second, 0.0, qt).astype(BF16)
        q_sc[1] = jnp.where(second, qt, 0.0).astype(BF16)

    def scores(k_off, k_len, c):
        s = jnp.dot(k_ref[pl.ds(k_off, k_len), :], q_sc[c], preferred_element_type=F32)
        if fox:
            ck = ck_ref[c, pl.ds(k_off, k_len), :]
            s = jnp.concatenate(
                [s[:, b * UNIT:(b + 1) * UNIT] - ck for b in range(tq // UNIT)], axis=1)
        return s

    def issue_scores(j, par):
        for c in range(2):
            s = scores(tile_start(j), tk, c)
            s_buf[par][c] = s
            mblk_buf[par][c] = jnp.max(s, axis=0, keepdims=True)

    def softmax_core(i, c, s, m_blk, m_old):
        if fox:
            cq = cq_ref[c:c + 1, pl.ds(tile_start(i), tq)]
            m_new = jnp.maximum(m_old, m_blk + cq)
            shift = m_new - cq
        else:
            m_new = jnp.maximum(m_old, m_blk)
            shift = m_new
        return m_new, jnp.exp2(m_old - m_new), jnp.exp2(s - shift)

    def softmax(i, par, mask):
        for c in range(2):
            s = s_buf[par][c]
            if mask is None:
                m_blk = mblk_buf[par][c]
            else:
                s = jnp.where(mask, s, NEG)
                m_blk = jnp.max(s, axis=0, keepdims=True)
            m_new, alpha, p = softmax_core(i, c, s, m_blk, m_sc[c])
            alpha_buf[par][c] = alpha
            p_buf[par][c] = p.astype(BF16)
            m_sc[c] = m_new

    def values_t(k_off, k_len):
        return jnp.concatenate(
            [vt_ref[:, pl.ds(k_off, k_len)], jnp.ones((ONES_ROWS, k_len), BF16)], axis=0)

    def pv_update(j, par):
        vtt = values_t(tile_start(j), tk)
        for c in range(2):
            acc_sc[c] = alpha_buf[par][c] * acc_sc[c] + jnp.dot(
                vtt, p_buf[par][c], preferred_element_type=F32)

    def start_row(i, causal):
        key = lax.broadcasted_iota(jnp.int32, (UNIT, tq), 0)
        mask = key < N_META
        if causal:
            mask = jnp.logical_and(mask, key <= lax.broadcasted_iota(jnp.int32, (UNIT, tq), 1))
        vtt = vt_ref[:, pl.ds(seq, UNIT)]
        for c in range(2):
            s = jnp.where(mask, scores(seq, UNIT, c), NEG)
            m_new, _, p = softmax_core(i, c, s, jnp.max(s, axis=0, keepdims=True),
                                       jnp.full((1, tq), NEG, F32))
            m_sc[c] = m_new
            p = p.astype(BF16)
            acc0_sc[c, 0:UNIT] = jnp.dot(vtt, p, preferred_element_type=F32)
            denom = jnp.sum(p.astype(F32), axis=0, keepdims=True)
            acc0_sc[c, UNIT:UNIT + ONES_ROWS] = jnp.broadcast_to(denom, (ONES_ROWS, tq))

    def normalized(c):
        acc = acc_sc[c]
        return acc[:UNIT] * (1.0 / acc[UNIT:UNIT + 1])

    def finalize(i):
        o1 = normalized(0)
        o2 = normalized(1)
        if fox:
            o = jnp.concatenate([o1[:FOX_HEAD_DIM], o2[FOX_HEAD_DIM:]], axis=0).T
        else:
            lam = (jnp.exp(jnp.sum(lam_ref[0:1, :] * lam_ref[1:2, :], axis=-1, keepdims=True))
                   - jnp.exp(jnp.sum(lam_ref[2:3, :] * lam_ref[3:4, :], axis=-1, keepdims=True))
                   + lambda_init)
            o = (o1 - lam * o2).T
            o = o * lax.rsqrt(jnp.mean(o * o, axis=-1, keepdims=True) + NORM_EPS) * g_ref[...]
            o = o * (1.0 - lambda_init)
        rows = pl.ds(tile_start(i), tq)
        o_ref[rows, :] = (o * z_ref[rows, :]).astype(BF16)

    def causal_mask():
        return (lax.broadcasted_iota(jnp.int32, (tk, tq), 0)
                <= lax.broadcasted_iota(jnp.int32, (tk, tq), 1))

    def diag_block(i, par, last):
        pv_update(jnp.maximum(i - 1, 0), 1 - par)
        if not last:
            load_queries(i + 1)
            issue_scores(0, 1 - par)
        softmax(i, par, causal_mask())
        if not last:
            start_row(i + 1, causal=False)

    def step(i, j, par):
        @pl.when(jnp.logical_and(j > 0, j < i))
        def _():
            softmax(i, par, None)
            pv_update(j - 1, 1 - par)
            issue_scores(j + 1, 1 - par)

        @pl.when(jnp.logical_and(j == 0, i > 0))
        def _():
            pv_update(i - 1, 1 - par)
            finalize(i - 1)
            acc_sc[...] = acc0_sc[...]
            softmax(i, par, None)
            issue_scores(1, 1 - par)

        @pl.when(j == i)
        def _():
            diag_block(i, par, last=False)

        wrap = j == i
        return jnp.where(wrap, i + 1, i), jnp.where(wrap, 0, j + 1)

    load_queries(0)
    start_row(0, causal=False)
    acc_sc[...] = acc0_sc[...]
    issue_scores(0, 0)
    alpha_buf[1][...] = jnp.ones_like(alpha_buf[1])
    p_buf[1][...] = jnp.zeros_like(p_buf[1])

    def body(_, ij):
        i, j = step(ij[0], ij[1], 0)
        return step(i, j, 1)

    i, j = lax.fori_loop(0, (n_steps - 2) // 2, body, (jnp.int32(0), jnp.int32(0)))
    step(i, j, 0)
    diag_block(nq - 1, 1, last=True)
    pv_update(nq - 1, 1)
    finalize(nq - 1)

    load_queries(nq)
    start_row(nq, causal=True)
    acc_sc[...] = acc0_sc[...]
    finalize(nq)


def _flash(qt, k, vt, zg, extra, *, fox, tq, seq, lambda_init):
    bsz, _, lp, _ = k.shape
    nq = seq // tq
    assert lp == seq + tq and (nq * (nq + 1) // 2) % 2 == 0
    z_off = N_UNITS if fox else 0
    q_spec = pl.BlockSpec((None, None, UNIT, lp), lambda b, u: (b, u, 0, 0))
    k_spec = pl.BlockSpec((None, None, lp, UNIT), lambda b, u: (b, u, 0, 0))
    z_spec = pl.BlockSpec((None, lp, UNIT), lambda b, u: (b, 0, u + z_off))
    if fox:
        extra_specs = [pl.BlockSpec((None, None, 2, lp), lambda b, u: (b, u, 0, 0)),
                       pl.BlockSpec((None, None, 2, lp, UNIT), lambda b, u: (b, u, 0, 0, 0))]
    else:
        extra_specs = [pl.BlockSpec((4, DIFF_QK_DIM), lambda b, u: (0, 0)),
                       pl.BlockSpec((None, 1, UNIT), lambda b, u: (u, 0, 0))]
    stat = pltpu.VMEM((2, 1, tq), F32)
    acc = pltpu.VMEM((2, UNIT + ONES_ROWS, tq), F32)
    score = pltpu.VMEM((2, tq, tq), F32)
    prob = pltpu.VMEM((2, tq, tq), BF16)
    return pl.pallas_call(
        functools.partial(_flash_kernel, fox=fox, tq=tq, seq=seq, lambda_init=lambda_init),
        grid=(bsz, N_UNITS),
        in_specs=[q_spec, k_spec, q_spec, z_spec] + extra_specs,
        out_specs=pl.BlockSpec((None, lp, UNIT), lambda b, u: (b, 0, u)),
        out_shape=jax.ShapeDtypeStruct((bsz, lp, N_UNITS * UNIT), BF16),
        scratch_shapes=[stat, acc, acc, pltpu.VMEM((2, UNIT, tq), BF16),
                        score, score, prob, prob, stat, stat, stat, stat],
        compiler_params=pltpu.CompilerParams(
            dimension_semantics=("parallel", "parallel"), vmem_limit_bytes=VMEM_LIMIT),
        name="fox_flash" if fox else "diff_flash",
    )(qt, k, vt, zg, *extra)


def _out_kernel(*refs, final):
    if final:
        ma_ref, mb_ref, w_ref, h_ref, g_ref, o_ref = refs
    else:
        ma_ref, mb_ref, w_ref, h_ref, o_ref = refs
    half = ma_ref.shape[-1]
    h = (h_ref[...]
         + jnp.dot(ma_ref[...], w_ref[0:half, :], preferred_element_type=F32)
         + jnp.dot(mb_ref[...], w_ref[half:2 * half, :], preferred_element_type=F32))
    if final:
        h = h * lax.rsqrt(jnp.mean(h * h, axis=-1, keepdims=True) + NORM_EPS) * g_ref[...]
    o_ref[...] = h


def _out_project(mixed_a, mixed_b, w_out, h, final_g, *, tl, rows):
    bsz, _, dm = h.shape
    half = mixed_a.shape[-1]
    final = final_g is not None
    row_spec = lambda width: pl.BlockSpec((None, tl, width), lambda b, i: (b, i, 0))
    const = lambda shape: pl.BlockSpec(shape, lambda b, i: (0,) * len(shape))
    in_specs = [row_spec(half), row_spec(half), const((2 * half, dm)), row_spec(dm)]
    args = [mixed_a, mixed_b, w_out, h]
    if final:
        in_specs.append(const((1, dm)))
        args.append(final_g)
    return pl.pallas_call(
        functools.partial(_out_kernel, final=final),
        grid=(bsz, rows // tl),
        in_specs=in_specs,
        out_specs=row_spec(dm),
        out_shape=jax.ShapeDtypeStruct((bsz, rows, dm), F32),
        compiler_params=pltpu.CompilerParams(
            dimension_semantics=("parallel", "parallel"), vmem_limit_bytes=VMEM_LIMIT),
        name="out_final" if final else "out_proj",
    )(*args)


def _rope_tables(seq, lp):
    half = DIFF_QK_DIM // 2
    pos = np.zeros((lp,), np.float64)
    pos[:seq] = N_META + np.arange(seq)
    pos[seq:seq + N_META] = np.arange(N_META)
    inv = ROPE_THETA ** (-np.arange(half, dtype=np.float64) / half)
    ang = pos[:, None] * np.tile(inv, UNIT // half)[None, :]
    sign = np.where(np.arange(UNIT) < UNIT // 2, -1.0, 1.0)[None, :]
    cos, sin = np.cos(ang), np.sin(ang) * sign
    q_scale = DIFF_QK_DIM ** -0.5 * LOG2E
    return tuple(jnp.asarray(t, F32) for t in (cos * q_scale, sin * q_scale, cos, sin))


def _unit_lane_order(w_qk):
    dm, width = w_qk.shape
    half = DIFF_QK_DIM // 2
    w5 = w_qk.reshape(dm, width // (4 * half), 2, 2, half)
    return w5.transpose(0, 1, 3, 2, 4).reshape(dm, width)


def _projection_weight(w_layer):
    width = N_UNITS * UNIT
    sec = lambda k: w_layer[:, k * width:(k + 1) * width]
    logits = jnp.pad(w_layer[:, 8 * width:], ((0, 0), (0, UNIT - FOX_HEADS)))
    return jnp.concatenate(
        [_unit_lane_order(sec(0)), _unit_lane_order(sec(1)), sec(2), sec(4), sec(5), sec(6),
         sec(3), sec(7), logits], axis=1).astype(BF16)


def kernel(x, meta_tokens, norm_g, w_in, b_forget, lam_q1, lam_k1, lam_q2, lam_k2, subln_g,
           w_out, final_g):
    bsz, seq, dm = x.shape
    depth = w_in.shape[0]
    tl = SEQ_TILE
    lp = seq + tl
    width = N_UNITS * UNIT
    assert seq % tl == 0 and N_META <= UNIT and w_in.shape[2] == 8 * width + FOX_HEADS

    meta = jnp.broadcast_to(meta_tokens.astype(x.dtype)[None], (bsz, N_META, dm))
    h = jnp.concatenate([x, meta, jnp.zeros((bsz, lp - seq - N_META, dm), x.dtype)], axis=1)

    tabs = _rope_tables(seq, lp)
    tri = jnp.asarray(np.tril(np.ones((tl, tl), np.float32)), BF16)

    out = None
    for layer in range(depth):
        lambda_init = 0.8 - 0.6 * math.exp(-0.3 * layer)
        w = _projection_weight(w_in[layer])
        bf = jnp.pad(b_forget[layer].astype(F32), (0, UNIT - FOX_HEADS)).reshape(1, UNIT)
        (qdt, kd, vdt, qft, kf, vft, zg, cqt, ckrep) = _project(
            h, norm_g[layer].reshape(1, dm), w, bf, tabs, tri, tl=tl)

        lam = jnp.stack([lam_q1[layer], lam_k1[layer], lam_q2[layer], lam_k2[layer]]).astype(F32)
        mixed_a = _flash(qdt, kd, vdt, zg, (lam, subln_g[layer].reshape(N_UNITS, 1, UNIT)),
                         fox=False, tq=tl, seq=seq, lambda_init=lambda_init)
        mixed_b = _flash(qft, kf, vft, zg,
                         (cqt.reshape(bsz, N_UNITS, 2, lp), ckrep.reshape(bsz, N_UNITS, 2, lp, UNIT)),
                         fox=True, tq=tl, seq=seq, lambda_init=lambda_init)

        w_o = w_out[layer].astype(BF16)
        if layer + 1 < depth:
            h = _out_project(mixed_a, mixed_b, w_o, h, None, tl=tl, rows=lp)
        else:
            out = _out_project(mixed_a, mixed_b, w_o, h, final_g.reshape(1, dm), tl=tl, rows=seq)
    return out
```

```python
import functools
import math

import numpy as np
import jax
import jax.numpy as jnp
from jax import lax
from jax.experimental import pallas as pl
from jax.experimental.pallas import tpu as pltpu

N_META = 16
DIFF_QK_DIM = 64
FOX_HEAD_DIM = 64
FOX_HEADS = 8
UNIT = 128
N_UNITS = 4
UNITS_PER_STEP = 2
ROPE_THETA = 10000.0
NORM_EPS = 1e-6
LOG2E = 1.0 / math.log(2.0)
NEG = -1e30
ONES_ROWS = 16

SEQ_TILE = 512
VMEM_LIMIT = 56 * 1024 * 1024

F32 = jnp.float32
BF16 = jnp.bfloat16


def _proj_kernel(h_ref, g_ref, w_ref, bf_ref, cosq_ref, sinq_ref, cosk_ref, sink_ref, tri_ref,
                 qdt_ref, kd_ref, vdt_ref, qft_ref, kf_ref, vft_ref, zg_ref, cum_ref,
                 carry_ref, *, tl, fox_scale):
    step = pl.program_id(1)

    h = h_ref[...]
    u = h * lax.rsqrt(jnp.mean(h * h, axis=-1, keepdims=True) + NORM_EPS) * g_ref[...]
    ub = u.astype(BF16)

    def proj(lo, width):
        return jnp.dot(ub, w_ref[:, lo:lo + width], preferred_element_type=F32)

    def rope(t, cos_ref, sin_ref):
        return t * cos_ref[...] + pltpu.roll(t, 64, 1) * sin_ref[...]

    width = N_UNITS * UNIT
    dq = proj(0 * width, width)
    for un in range(N_UNITS):
        qdt_ref[un] = rope(dq[:, un * UNIT:(un + 1) * UNIT], cosq_ref, sinq_ref).T.astype(BF16)
    dk = proj(1 * width, width)
    for un in range(N_UNITS):
        kd_ref[un] = rope(dk[:, un * UNIT:(un + 1) * UNIT], cosk_ref, sink_ref).astype(BF16)
    dv = proj(2 * width, width)
    for un in range(N_UNITS):
        vdt_ref[un] = dv[:, un * UNIT:(un + 1) * UNIT].T.astype(BF16)
    fq = proj(3 * width, width) * fox_scale
    for un in range(N_UNITS):
        qft_ref[un] = fq[:, un * UNIT:(un + 1) * UNIT].T.astype(BF16)
    fk = proj(4 * width, width)
    for un in range(N_UNITS):
        kf_ref[un] = fk[:, un * UNIT:(un + 1) * UNIT].astype(BF16)
    fv = proj(5 * width, width)
    for un in range(N_UNITS):
        vft_ref[un] = fv[:, un * UNIT:(un + 1) * UNIT].T.astype(BF16)

    z = proj(6 * width, 2 * width)
    zg_ref[...] = z * (1.0 / (1.0 + jnp.exp(-z)))

    fl = proj(8 * width, UNIT) + bf_ref[...]
    ls = (jnp.minimum(fl, 0.0) - jnp.log1p(jnp.exp(-jnp.abs(fl)))) * LOG2E
    row = lax.broadcasted_iota(jnp.int32, (tl, UNIT), 0)
    ls = jnp.where(row >= jnp.where(step == 0, N_META, tl), 0.0, ls)

    @pl.when(step == 0)
    def _():
        carry_ref[...] = jnp.zeros_like(carry_ref)

    hi = ls.astype(BF16)
    r1 = ls - hi.astype(F32)
    mid = r1.astype(BF16)
    lo = (r1 - mid.astype(F32)).astype(BF16)
    tri = tri_ref[...]
    cum = (jnp.dot(tri, hi, preferred_element_type=F32)
           + jnp.dot(tri, mid, preferred_element_type=F32)
           + jnp.dot(tri, lo, preferred_element_type=F32)) + carry_ref[...]
    carry_ref[...] = cum[tl - 1:tl, :]
    cum_ref[...] = cum.T[0:FOX_HEADS, :]


def _project(h, norm_g, w, b_forget, tabs, tri, *, tl):
    bsz, lp, dm = h.shape
    nl = lp // tl
    pw = w.shape[1]

    def tok(b, i):
        return (i + nl - 1) % nl

    row_spec = lambda width: pl.BlockSpec((None, tl, width), lambda b, i: (b, tok(b, i), 0))
    unit_rows = pl.BlockSpec((None, N_UNITS, tl, UNIT), lambda b, i: (b, 0, tok(b, i), 0))
    unit_cols = pl.BlockSpec((None, N_UNITS, UNIT, tl), lambda b, i: (b, 0, 0, tok(b, i)))
    tab_spec = pl.BlockSpec((tl, UNIT), lambda b, i: (tok(b, i), 0))
    const = lambda shape: pl.BlockSpec(shape, lambda b, i: (0,) * len(shape))

    unit_t = jax.ShapeDtypeStruct((bsz, N_UNITS, UNIT, lp), BF16)
    unit_n = jax.ShapeDtypeStruct((bsz, N_UNITS, lp, UNIT), BF16)
    return pl.pallas_call(
        functools.partial(_proj_kernel, tl=tl, fox_scale=FOX_HEAD_DIM ** -0.5 * LOG2E),
        grid=(bsz, nl),
        in_specs=[row_spec(dm), const((1, dm)), const((dm, pw)), const((1, UNIT)),
                  tab_spec, tab_spec, tab_spec, tab_spec, const((tl, tl))],
        out_specs=[unit_cols, unit_rows, unit_cols, unit_cols, unit_rows, unit_cols,
                   row_spec(2 * N_UNITS * UNIT),
                   pl.BlockSpec((None, FOX_HEADS, tl), lambda b, i: (b, 0, tok(b, i)))],
        out_shape=[unit_t, unit_n, unit_t, unit_t, unit_n, unit_t,
                   jax.ShapeDtypeStruct((bsz, lp, 2 * N_UNITS * UNIT), F32),
                   jax.ShapeDtypeStruct((bsz, FOX_HEADS, lp), F32)],
        scratch_shapes=[pltpu.VMEM((1, UNIT), F32)],
        compiler_params=pltpu.CompilerParams(
            dimension_semantics=("arbitrary", "arbitrary"), vmem_limit_bytes=VMEM_LIMIT),
        name="proj",
    )(h, norm_g, w, b_forget, *tabs, tri)


def _flash_kernel(*refs, fox, tq, seq, lambda_init):
    if fox:
        qt_ref, k_ref, vt_ref, cum_ref, o_ref = refs[:5]
        n_in = 5
    else:
        qt_ref, k_ref, vt_ref, lam_ref, g_ref, o_ref = refs[:6]
        n_in = 6
    m_sc, acc_sc, acc0_sc, q_sc = refs[n_in:n_in + 4]
    s_buf, p_buf, alpha_buf, mblk_buf = (
        refs[n_in + 4 + 2 * n:n_in + 6 + 2 * n] for n in range(4))
    units = range(UNITS_PER_STEP)
    tk = tq
    nq = seq // tq
    n_steps = nq * (nq + 1) // 2

    def tile_start(t):
        return pl.multiple_of(t * tq, tq)

    def load_queries(un, i):
        feat = lax.broadcasted_iota(jnp.int32, (UNIT, tq), 0)
        second = (feat >= FOX_HEAD_DIM) if fox else ((feat // 32) % 2 == 1)
        qt = qt_ref[un, :, pl.ds(tile_start(i), tq)].astype(F32)
        q_sc[un, 0] = jnp.where(second, 0.0, qt).astype(BF16)
        q_sc[un, 1] = jnp.where(second, qt, 0.0).astype(BF16)

    def scores(un, k_off, k_len, c):
        s = jnp.dot(k_ref[un, pl.ds(k_off, k_len), :], q_sc[un, c], preferred_element_type=F32)
        if fox:
            ck = jnp.broadcast_to(cum_ref[un, c:c + 1, pl.ds(k_off, k_len)], (UNIT, k_len)).T
            s = jnp.concatenate(
                [s[:, b * UNIT:(b + 1) * UNIT] - ck for b in range(tq // UNIT)], axis=1)
        return s

    def issue_scores(un, j, par):
        for c in range(2):
            s = scores(un, tile_start(j), tk, c)
            s_buf[par][un, c] = s
            mblk_buf[par][un, c] = jnp.max(s, axis=0, keepdims=True)

    def softmax_core(un, i, c, s, m_blk, m_old):
        if fox:
            cq = cum_ref[un, c:c + 1, pl.ds(tile_start(i), tq)]
            m_new = jnp.maximum(m_old, m_blk + cq)
            shift = m_new - cq
        else:
            m_new = jnp.maximum(m_old, m_blk)
            shift = m_new
        return m_new, jnp.exp2(m_old - m_new), jnp.exp2(s - shift)

    def softmax(un, i, par, mask):
        for c in range(2):
            s = s_buf[par][un, c]
            if mask is None:
                m_blk = mblk_buf[par][un, c]
            else:
                s = jnp.where(mask, s, NEG)
                m_blk = jnp.max(s, axis=0, keepdims=True)
            m_new, alpha, p = softmax_core(un, i, c, s, m_blk, m_sc[un, c])
            alpha_buf[par][un, c] = alpha
            p_buf[par][un, c] = p.astype(BF16)
            m_sc[un, c] = m_new

    def pv_update(un, j, par):
        vtt = jnp.concatenate(
            [vt_ref[un, :, pl.ds(tile_start(j), tk)], jnp.ones((ONES_ROWS, tk), BF16)], axis=0)
        for c in range(2):
            acc_sc[un, c] = alpha_buf[par][un, c] * acc_sc[un, c] + jnp.dot(
                vtt, p_buf[par][un, c], preferred_element_type=F32)

    def start_row(un, i, causal):
        key = lax.broadcasted_iota(jnp.int32, (UNIT, tq), 0)
        mask = key < N_META
        if causal:
            mask = jnp.logical_and(mask, key <= lax.broadcasted_iota(jnp.int32, (UNIT, tq), 1))
        vtt = vt_ref[un, :, pl.ds(seq, UNIT)]
        for c in range(2):
            s = jnp.where(mask, scores(un, seq, UNIT, c), NEG)
            m_new, _, p = softmax_core(un, i, c, s, jnp.max(s, axis=0, keepdims=True),
                                       jnp.full((1, tq), NEG, F32))
            m_sc[un, c] = m_new
            p = p.astype(BF16)
            acc0_sc[un, c, 0:UNIT] = jnp.dot(vtt, p, preferred_element_type=F32)
            denom = jnp.sum(p.astype(F32), axis=0, keepdims=True)
            acc0_sc[un, c, UNIT:UNIT + ONES_ROWS] = jnp.broadcast_to(denom, (ONES_ROWS, tq))

    def normalized(un, c):
        acc = acc_sc[un, c]
        return acc[:UNIT] * (1.0 / acc[UNIT:UNIT + 1])

    def finalize(un, i):
        o1 = normalized(un, 0)
        o2 = normalized(un, 1)
        if fox:
            o = jnp.concatenate([o1[:FOX_HEAD_DIM], o2[FOX_HEAD_DIM:]], axis=0).T
        else:
            lam = (jnp.exp(jnp.sum(lam_ref[0:1, :] * lam_ref[1:2, :], axis=-1, keepdims=True))
                   - jnp.exp(jnp.sum(lam_ref[2:3, :] * lam_ref[3:4, :], axis=-1, keepdims=True))
                   + lambda_init)
            o = (o1 - lam * o2).T
            o = o * lax.rsqrt(jnp.mean(o * o, axis=-1, keepdims=True) + NORM_EPS) * g_ref[un]
            o = o * (1.0 - lambda_init)
        o_ref[pl.ds(tile_start(i), tq), un * UNIT:(un + 1) * UNIT] = o.astype(BF16)

    def causal_mask():
        return (lax.broadcasted_iota(jnp.int32, (tk, tq), 0)
                <= lax.broadcasted_iota(jnp.int32, (tk, tq), 1))

    def diag_block(i, par, last):
        for un in units:
            pv_update(un, jnp.maximum(i - 1, 0), 1 - par)
        if not last:
            for un in units:
                load_queries(un, i + 1)
                issue_scores(un, 0, 1 - par)
        for un in units:
            softmax(un, i, par, causal_mask())
        if not last:
            for un in units:
                start_row(un, i + 1, causal=False)

    def step(i, j, par):
        @pl.when(jnp.logical_and(j > 0, j < i))
        def _():
            for un in units:
                softmax(un, i, par, None)
                pv_update(un, j - 1, 1 - par)
                issue_scores(un, j + 1, 1 - par)

        @pl.when(jnp.logical_and(j == 0, i > 0))
        def _():
            for un in units:
                pv_update(un, i - 1, 1 - par)
                finalize(un, i - 1)
                acc_sc[un] = acc0_sc[un]
            for un in units:
                softmax(un, i, par, None)
            for un in units:
                issue_scores(un, 1, 1 - par)

        @pl.when(j == i)
        def _():
            diag_block(i, par, last=False)

        wrap = j == i
        return jnp.where(wrap, i + 1, i), jnp.where(wrap, 0, j + 1)

    for un in units:
        load_queries(un, 0)
        start_row(un, 0, causal=False)
        acc_sc[un] = acc0_sc[un]
        issue_scores(un, 0, 0)
    alpha_buf[1][...] = jnp.ones_like(alpha_buf[1])
    p_buf[1][...] = jnp.zeros_like(p_buf[1])

    def body(_, ij):
        i, j = step(ij[0], ij[1], 0)
        return step(i, j, 1)

    i, j = lax.fori_loop(0, (n_steps - 2) // 2, body, (jnp.int32(0), jnp.int32(0)))
    step(i, j, 0)
    diag_block(nq - 1, 1, last=True)
    for un in units:
        pv_update(un, nq - 1, 1)
        finalize(un, nq - 1)

    for un in units:
        load_queries(un, nq)
        start_row(un, nq, causal=True)
        acc_sc[un] = acc0_sc[un]
        finalize(un, nq)


def _flash(qt, k, vt, extra, *, fox, tq, seq, lambda_init):
    bsz, _, lp, _ = k.shape
    nq = seq // tq
    ups = UNITS_PER_STEP
    assert lp == seq + tq and (nq * (nq + 1) // 2) % 2 == 0 and N_UNITS % ups == 0
    q_spec = pl.BlockSpec((None, ups, UNIT, lp), lambda b, u: (b, u, 0, 0))
    k_spec = pl.BlockSpec((None, ups, lp, UNIT), lambda b, u: (b, u, 0, 0))
    if fox:
        extra_specs = [pl.BlockSpec((None, ups, 2, lp), lambda b, u: (b, u, 0, 0))]
    else:
        extra_specs = [pl.BlockSpec((4, DIFF_QK_DIM), lambda b, u: (0, 0)),
                       pl.BlockSpec((ups, 1, UNIT), lambda b, u: (u, 0, 0))]
    stat = pltpu.VMEM((ups, 2, 1, tq), F32)
    acc = pltpu.VMEM((ups, 2, UNIT + ONES_ROWS, tq), F32)
    score = pltpu.VMEM((ups, 2, tq, tq), F32)
    prob = pltpu.VMEM((ups, 2, tq, tq), BF16)
    return pl.pallas_call(
        functools.partial(_flash_kernel, fox=fox, tq=tq, seq=seq, lambda_init=lambda_init),
        grid=(bsz, N_UNITS // ups),
        in_specs=[q_spec, k_spec, q_spec] + extra_specs,
        out_specs=pl.BlockSpec((None, lp, ups * UNIT), lambda b, u: (b, 0, u)),
        out_shape=jax.ShapeDtypeStruct((bsz, lp, N_UNITS * UNIT), BF16),
        scratch_shapes=[stat, acc, acc, pltpu.VMEM((ups, 2, UNIT, tq), BF16),
                        score, score, prob, prob, stat, stat, stat, stat],
        compiler_params=pltpu.CompilerParams(
            dimension_semantics=("parallel", "parallel"), vmem_limit_bytes=VMEM_LIMIT),
        name="fox_flash" if fox else "diff_flash",
    )(qt, k, vt, *extra)


def _out_kernel(*refs, final):
    if final:
        oa_ref, ob_ref, z_ref, w_ref, h_ref, g_ref, o_ref = refs
    else:
        oa_ref, ob_ref, z_ref, w_ref, h_ref, o_ref = refs
    half = oa_ref.shape[-1]
    mixed_a = (oa_ref[...].astype(F32) * z_ref[:, 0:half]).astype(BF16)
    mixed_b = (ob_ref[...].astype(F32) * z_ref[:, half:2 * half]).astype(BF16)
    h = (h_ref[...]
         + jnp.dot(mixed_a, w_ref[0:half, :], preferred_element_type=F32)
         + jnp.dot(mixed_b, w_ref[half:2 * half, :], preferred_element_type=F32))
    if final:
        h = h * lax.rsqrt(jnp.mean(h * h, axis=-1, keepdims=True) + NORM_EPS) * g_ref[...]
    o_ref[...] = h


def _out_project(o_a, o_b, zg, w_out, h, final_g, *, tl, rows):
    bsz, _, dm = h.shape
    half = o_a.shape[-1]
    final = final_g is not None
    row_spec = lambda width: pl.BlockSpec((None, tl, width), lambda b, i: (b, i, 0))
    const = lambda shape: pl.BlockSpec(shape, lambda b, i: (0,) * len(shape))
    in_specs = [row_spec(half), row_spec(half), row_spec(2 * half), const((2 * half, dm)),
                row_spec(dm)]
    args = [o_a, o_b, zg, w_out, h]
    if final:
        in_specs.append(const((1, dm)))
        args.append(final_g)
    return pl.pallas_call(
        functools.partial(_out_kernel, final=final),
        grid=(bsz, rows // tl),
        in_specs=in_specs,
        out_specs=row_spec(dm),
        out_shape=jax.ShapeDtypeStruct((bsz, rows, dm), F32),
        compiler_params=pltpu.CompilerParams(
            dimension_semantics=("parallel", "parallel"), vmem_limit_bytes=VMEM_LIMIT),
        name="out_final" if final else "out_proj",
    )(*args)


def _rope_tables(seq, lp):
    half = DIFF_QK_DIM // 2
    pos = np.zeros((lp,), np.float64)
    pos[:seq] = N_META + np.arange(seq)
    pos[seq:seq + N_META] = np.arange(N_META)
    inv = ROPE_THETA ** (-np.arange(half, dtype=np.float64) / half)
    ang = pos[:, None] * np.tile(inv, UNIT // half)[None, :]
    sign = np.where(np.arange(UNIT) < UNIT // 2, -1.0, 1.0)[None, :]
    cos, sin = np.cos(ang), np.sin(ang) * sign
    q_scale = DIFF_QK_DIM ** -0.5 * LOG2E
    return tuple(jnp.asarray(t, F32) for t in (cos * q_scale, sin * q_scale, cos, sin))


def _unit_lane_order(w_qk):
    dm, width = w_qk.shape
    half = DIFF_QK_DIM // 2
    w5 = w_qk.reshape(dm, width // (4 * half), 2, 2, half)
    return w5.transpose(0, 1, 3, 2, 4).reshape(dm, width)


def _projection_weight(w_layer):
    width = N_UNITS * UNIT
    sec = lambda k: w_layer[:, k * width:(k + 1) * width]
    logits = jnp.pad(w_layer[:, 8 * width:], ((0, 0), (0, UNIT - FOX_HEADS)))
    return jnp.concatenate(
        [_unit_lane_order(sec(0)), _unit_lane_order(sec(1)), sec(2), sec(4), sec(5), sec(6),
         sec(3), sec(7), logits], axis=1).astype(BF16)


def kernel(x, meta_tokens, norm_g, w_in, b_forget, lam_q1, lam_k1, lam_q2, lam_k2, subln_g,
           w_out, final_g):
    bsz, seq, dm = x.shape
    depth = w_in.shape[0]
    tl = SEQ_TILE
    lp = seq + tl
    width = N_UNITS * UNIT
    assert seq % tl == 0 and N_META <= UNIT and w_in.shape[2] == 8 * width + FOX_HEADS

    meta = jnp.broadcast_to(meta_tokens.astype(x.dtype)[None], (bsz, N_META, dm))
    h = jnp.concatenate([x, meta, jnp.zeros((bsz, lp - seq - N_META, dm), x.dtype)], axis=1)

    tabs = _rope_tables(seq, lp)
    tri = jnp.asarray(np.tril(np.ones((tl, tl), np.float32)), BF16)

    out = None
    for layer in range(depth):
        lambda_init = 0.8 - 0.6 * math.exp(-0.3 * layer)
        w = _projection_weight(w_in[layer])
        bf = jnp.pad(b_forget[layer].astype(F32), (0, UNIT - FOX_HEADS)).reshape(1, UNIT)
        (qdt, kd, vdt, qft, kf, vft, zg, cum) = _project(
            h, norm_g[layer].reshape(1, dm), w, bf, tabs, tri, tl=tl)

        lam = jnp.stack([lam_q1[layer], lam_k1[layer], lam_q2[layer], lam_k2[layer]]).astype(F32)
        o_a = _flash(qdt, kd, vdt, (lam, subln_g[layer].reshape(N_UNITS, 1, UNIT)),
                     fox=False, tq=tl, seq=seq, lambda_init=lambda_init)
        o_b = _flash(qft, kf, vft, (cum.reshape(bsz, N_UNITS, 2, lp),),
                     fox=True, tq=tl, seq=seq, lambda_init=lambda_init)

        w_o = w_out[layer].astype(BF16)
        if layer + 1 < depth:
            h = _out_project(o_a, o_b, zg, w_o, h, None, tl=tl, rows=lp)
        else:
            out = _out_project(o_a, o_b, zg, w_o, h, final_g.reshape(1, dm), tl=tl, rows=seq)
    return out
```

```python
import functools
import math

import numpy as np
import jax
import jax.numpy as jnp
from jax import lax
from jax.experimental import pallas as pl
from jax.experimental.pallas import tpu as pltpu

N_META = 16
DIFF_QK_DIM = 64
FOX_HEAD_DIM = 64
FOX_HEADS = 8
UNIT = 128
N_UNITS = 4
UNITS_PER_STEP = 2
ROPE_THETA = 10000.0
NORM_EPS = 1e-6
LOG2E = 1.0 / math.log(2.0)
NEG = -1e30
ONES_ROWS = 16

SEQ_TILE = 512
VMEM_LIMIT = 56 * 1024 * 1024

F32 = jnp.float32
BF16 = jnp.bfloat16


def _hidden_tile(h_ref, meta_ref, is_meta_tile):
    if meta_ref is None:
        return h_ref[...]
    return jnp.where(is_meta_tile, meta_ref[...], h_ref[...])


def _proj_kernel(*refs, tl, fox_scale, split_meta):
    h_ref, meta_ref = (refs[0], refs[1]) if split_meta else (refs[0], None)
    (g_ref, w_ref, bf_ref, cosq_ref, sinq_ref, cosk_ref, sink_ref, tri_ref,
     qdt_ref, kd_ref, vdt_ref, qft_ref, kf_ref, vft_ref, zg_ref, cum_ref,
     carry_ref) = refs[2 if split_meta else 1:]
    step = pl.program_id(1)

    h = _hidden_tile(h_ref, meta_ref, step == 0)
    u = h * lax.rsqrt(jnp.mean(h * h, axis=-1, keepdims=True) + NORM_EPS) * g_ref[...]
    ub = u.astype(BF16)

    def proj(lo, width):
        return jnp.dot(ub, w_ref[:, lo:lo + width], preferred_element_type=F32)

    def rope(t, cos_ref, sin_ref):
        return t * cos_ref[...] + pltpu.roll(t, 64, 1) * sin_ref[...]

    width = N_UNITS * UNIT
    dq = proj(0 * width, width)
    for un in range(N_UNITS):
        qdt_ref[un] = rope(dq[:, un * UNIT:(un + 1) * UNIT], cosq_ref, sinq_ref).T.astype(BF16)
    dk = proj(1 * width, width)
    for un in range(N_UNITS):
        kd_ref[un] = rope(dk[:, un * UNIT:(un + 1) * UNIT], cosk_ref, sink_ref).astype(BF16)
    dv = proj(2 * width, width)
    for un in range(N_UNITS):
        vdt_ref[un] = dv[:, un * UNIT:(un + 1) * UNIT].T.astype(BF16)
    fq = proj(3 * width, width) * fox_scale
    for un in range(N_UNITS):
        qft_ref[un] = fq[:, un * UNIT:(un + 1) * UNIT].T.astype(BF16)
    fk = proj(4 * width, width)
    for un in range(N_UNITS):
        kf_ref[un] = fk[:, un * UNIT:(un + 1) * UNIT].astype(BF16)
    fv = proj(5 * width, width)
    for un in range(N_UNITS):
        vft_ref[un] = fv[:, un * UNIT:(un + 1) * UNIT].T.astype(BF16)

    z = proj(6 * width, 2 * width)
    zg_ref[...] = z * (1.0 / (1.0 + jnp.exp(-z)))

    fl = proj(8 * width, UNIT) + bf_ref[...]
    ls = (jnp.minimum(fl, 0.0) - jnp.log1p(jnp.exp(-jnp.abs(fl)))) * LOG2E
    row = lax.broadcasted_iota(jnp.int32, (tl, UNIT), 0)
    ls = jnp.where(row >= jnp.where(step == 0, N_META, tl), 0.0, ls)

    @pl.when(step == 0)
    def _():
        carry_ref[...] = jnp.zeros_like(carry_ref)

    hi = ls.astype(BF16)
    r1 = ls - hi.astype(F32)
    mid = r1.astype(BF16)
    lo = (r1 - mid.astype(F32)).astype(BF16)
    tri = tri_ref[...]
    cum = (jnp.dot(tri, hi, preferred_element_type=F32)
           + jnp.dot(tri, mid, preferred_element_type=F32)
           + jnp.dot(tri, lo, preferred_element_type=F32)) + carry_ref[...]
    carry_ref[...] = cum[tl - 1:tl, :]
    cum_ref[...] = cum.T[0:FOX_HEADS, :]


def _project(h, meta_tile, norm_g, w, b_forget, tabs, tri, *, tl, lp):
    bsz, _, dm = h.shape
    nl = lp // tl
    pw = w.shape[1]
    split_meta = meta_tile is not None

    def tok(b, i):
        return (i + nl - 1) % nl

    row_spec = lambda width: pl.BlockSpec((None, tl, width), lambda b, i: (b, tok(b, i), 0))
    unit_rows = pl.BlockSpec((None, N_UNITS, tl, UNIT), lambda b, i: (b, 0, tok(b, i), 0))
    unit_cols = pl.BlockSpec((None, N_UNITS, UNIT, tl), lambda b, i: (b, 0, 0, tok(b, i)))
    tab_spec = pl.BlockSpec((tl, UNIT), lambda b, i: (tok(b, i), 0))
    const = lambda shape: pl.BlockSpec(shape, lambda b, i: (0,) * len(shape))

    unit_t = jax.ShapeDtypeStruct((bsz, N_UNITS, UNIT, lp), BF16)
    unit_n = jax.ShapeDtypeStruct((bsz, N_UNITS, lp, UNIT), BF16)
    if split_meta:
        h_specs = [pl.BlockSpec((None, tl, dm), lambda b, i: (b, jnp.minimum(tok(b, i), nl - 2), 0)),
                   const((tl, dm))]
        h_args = [h, meta_tile]
    else:
        h_specs, h_args = [row_spec(dm)], [h]
    return pl.pallas_call(
        functools.partial(_proj_kernel, tl=tl, fox_scale=FOX_HEAD_DIM ** -0.5 * LOG2E,
                          split_meta=split_meta),
        grid=(bsz, nl),
        in_specs=h_specs + [const((1, dm)), const((dm, pw)), const((1, UNIT)),
                            tab_spec, tab_spec, tab_spec, tab_spec, const((tl, tl))],
        out_specs=[unit_cols, unit_rows, unit_cols, unit_cols, unit_rows, unit_cols,
                   row_spec(2 * N_UNITS * UNIT),
                   pl.BlockSpec((None, FOX_HEADS, tl), lambda b, i: (b, 0, tok(b, i)))],
        out_shape=[unit_t, unit_n, unit_t, unit_t, unit_n, unit_t,
                   jax.ShapeDtypeStruct((bsz, lp, 2 * N_UNITS * UNIT), F32),
                   jax.ShapeDtypeStruct((bsz, FOX_HEADS, lp), F32)],
        scratch_shapes=[pltpu.VMEM((1, UNIT), F32)],
        compiler_params=pltpu.CompilerParams(
            dimension_semantics=("arbitrary", "arbitrary"), vmem_limit_bytes=VMEM_LIMIT),
        name="proj",
    )(*h_args, norm_g, w, b_forget, *tabs, tri)


def _flash_kernel(*refs, fox, tq, seq, lambda_init):
    if fox:
        qt_ref, k_ref, vt_ref, cum_ref, o_ref = refs[:5]
        n_in = 5
    else:
        qt_ref, k_ref, vt_ref, lam_ref, g_ref, o_ref = refs[:6]
        n_in = 6
    m_sc, acc_sc, acc0_sc, q_sc = refs[n_in:n_in + 4]
    s_buf, p_buf, alpha_buf, mblk_buf = (
        refs[n_in + 4 + 2 * n:n_in + 6 + 2 * n] for n in range(4))
    units = range(UNITS_PER_STEP)
    tk = tq
    nq = seq // tq
    n_steps = nq * (nq + 1) // 2

    def tile_start(t):
        return pl.multiple_of(t * tq, tq)

    def load_queries(un, i):
        feat = lax.broadcasted_iota(jnp.int32, (UNIT, tq), 0)
        second = (feat >= FOX_HEAD_DIM) if fox else ((feat // 32) % 2 == 1)
        qt = qt_ref[un, :, pl.ds(tile_start(i), tq)].astype(F32)
        q_sc[un, 0] = jnp.where(second, 0.0, qt).astype(BF16)
        q_sc[un, 1] = jnp.where(second, qt, 0.0).astype(BF16)

    def scores(un, k_off, k_len, c):
        s = jnp.dot(k_ref[un, pl.ds(k_off, k_len), :], q_sc[un, c], preferred_element_type=F32)
        if fox:
            ck = jnp.broadcast_to(cum_ref[un, c:c + 1, pl.ds(k_off, k_len)], (UNIT, k_len)).T
            s = jnp.concatenate(
                [s[:, b * UNIT:(b + 1) * UNIT] - ck for b in range(tq // UNIT)], axis=1)
        return s

    def issue_scores(un, j, par):
        for c in range(2):
            s = scores(un, tile_start(j), tk, c)
            s_buf[par][un, c] = s
            mblk_buf[par][un, c] = jnp.max(s, axis=0, keepdims=True)

    def softmax_core(un, i, c, s, m_blk, m_old):
        if fox:
            cq = cum_ref[un, c:c + 1, pl.ds(tile_start(i), tq)]
            m_new = jnp.maximum(m_old, m_blk + cq)
            shift = m_new - cq
        else:
            m_new = jnp.maximum(m_old, m_blk)
            shift = m_new
        return m_new, jnp.exp2(m_old - m_new), jnp.exp2(s - shift)

    def softmax(un, i, par, diagonal=False):
        for c in range(2):
            s = s_buf[par][un, c]
            if diagonal:
                s = jnp.where(lax.broadcasted_iota(jnp.int32, (tk, tq), 0)
                              <= lax.broadcasted_iota(jnp.int32, (tk, tq), 1), s, NEG)
                m_blk = jnp.max(s, axis=0, keepdims=True)
            else:
                m_blk = mblk_buf[par][un, c]
            m_new, alpha, p = softmax_core(un, i, c, s, m_blk, m_sc[un, c])
            alpha_buf[par][un, c] = alpha
            p_buf[par][un, c] = p.astype(BF16)
            m_sc[un, c] = m_new

    def pv_update(un, j, par):
        vtt = jnp.concatenate(
            [vt_ref[un, :, pl.ds(tile_start(j), tk)], jnp.ones((ONES_ROWS, tk), BF16)], axis=0)
        for c in range(2):
            acc_sc[un, c] = alpha_buf[par][un, c] * acc_sc[un, c] + jnp.dot(
                vtt, p_buf[par][un, c], preferred_element_type=F32)

    def meta_scores(un, causal):
        key = lax.broadcasted_iota(jnp.int32, (UNIT, tq), 0)
        mask = key < N_META
        if causal:
            mask = jnp.logical_and(mask, key <= lax.broadcasted_iota(jnp.int32, (UNIT, tq), 1))
        return [jnp.where(mask, scores(un, seq, UNIT, c), NEG) for c in range(2)]

    def meta_softmax(un, i, s_pair):
        p_pair = []
        for c, s in enumerate(s_pair):
            m_new, _, p = softmax_core(un, i, c, s, jnp.max(s, axis=0, keepdims=True),
                                       jnp.full((1, tq), NEG, F32))
            m_sc[un, c] = m_new
            p_pair.append(p.astype(BF16))
        return p_pair

    def meta_values(un, p_pair):
        vtt = vt_ref[un, :, pl.ds(seq, UNIT)]
        for c, p in enumerate(p_pair):
            acc0_sc[un, c, 0:UNIT] = jnp.dot(vtt, p, preferred_element_type=F32)
            denom = jnp.sum(p.astype(F32), axis=0, keepdims=True)
            acc0_sc[un, c, UNIT:UNIT + ONES_ROWS] = jnp.broadcast_to(denom, (ONES_ROWS, tq))

    def start_row(un, i, causal):
        meta_values(un, meta_softmax(un, i, meta_scores(un, causal)))

    def normalized(un, c):
        acc = acc_sc[un, c]
        return acc[:UNIT] * (1.0 / acc[UNIT:UNIT + 1])

    def finalize(un, i):
        o1 = normalized(un, 0)
        o2 = normalized(un, 1)
        if fox:
            o = jnp.concatenate([o1[:FOX_HEAD_DIM], o2[FOX_HEAD_DIM:]], axis=0).T
        else:
            lam = (jnp.exp(jnp.sum(lam_ref[0:1, :] * lam_ref[1:2, :], axis=-1, keepdims=True))
                   - jnp.exp(jnp.sum(lam_ref[2:3, :] * lam_ref[3:4, :], axis=-1, keepdims=True))
                   + lambda_init)
            o = (o1 - lam * o2).T
            o = o * lax.rsqrt(jnp.mean(o * o, axis=-1, keepdims=True) + NORM_EPS) * g_ref[un]
            o = o * (1.0 - lambda_init)
        o_ref[pl.ds(tile_start(i), tq), un * UNIT:(un + 1) * UNIT] = o.astype(BF16)

    def diag_softmax(i, par):
        for un in units:
            softmax(un, i, par, diagonal=True)
            pv_update(un, jnp.maximum(i - 1, 0), 1 - par)

    def next_row(i, par):
        for un in units:
            load_queries(un, i + 1)
        s_meta = [meta_scores(un, causal=False) for un in units]
        for un in units:
            issue_scores(un, 0, 1 - par)
        p_meta = [meta_softmax(un, i + 1, s_meta[un]) for un in units]
        for un in units:
            meta_values(un, p_meta[un])

    def step(i, j, par):
        @pl.when(jnp.logical_and(j > 0, j < i))
        def _():
            for un in units:
                softmax(un, i, par)
                pv_update(un, j - 1, 1 - par)
                issue_scores(un, j + 1, 1 - par)

        @pl.when(jnp.logical_and(j == 0, i > 0))
        def _():
            for un in units:
                softmax(un, i, par)
                pv_update(un, i - 1, 1 - par)
            for un in units:
                issue_scores(un, 1, 1 - par)
                finalize(un, i - 1)
                acc_sc[un] = acc0_sc[un]

        @pl.when(j == i)
        def _():
            diag_softmax(i, par)
            next_row(i, par)

        wrap = j == i
        return jnp.where(wrap, i + 1, i), jnp.where(wrap, 0, j + 1)

    for un in units:
        load_queries(un, 0)
        start_row(un, 0, causal=False)
        acc_sc[un] = acc0_sc[un]
        issue_scores(un, 0, 0)
    alpha_buf[1][...] = jnp.ones_like(alpha_buf[1])
    p_buf[1][...] = jnp.zeros_like(p_buf[1])

    def body(_, ij):
        i, j = step(ij[0], ij[1], 0)
        return step(i, j, 1)

    i, j = lax.fori_loop(0, (n_steps - 2) // 2, body, (jnp.int32(0), jnp.int32(0)))
    step(i, j, 0)
    diag_softmax(nq - 1, 1)
    for un in units:
        pv_update(un, nq - 1, 1)
        finalize(un, nq - 1)

    for un in units:
        load_queries(un, nq)
        start_row(un, nq, causal=True)
        acc_sc[un] = acc0_sc[un]
        finalize(un, nq)


def _flash(qt, k, vt, extra, *, fox, tq, seq, lambda_init):
    bsz, _, lp, _ = k.shape
    nq = seq // tq
    ups = UNITS_PER_STEP
    assert lp == seq + tq and (nq * (nq + 1) // 2) % 2 == 0 and N_UNITS % ups == 0
    q_spec = pl.BlockSpec((None, ups, UNIT, lp), lambda b, u: (b, u, 0, 0))
    k_spec = pl.BlockSpec((None, ups, lp, UNIT), lambda b, u: (b, u, 0, 0))
    if fox:
        extra_specs = [pl.BlockSpec((None, ups, 2, lp), lambda b, u: (b, u, 0, 0))]
    else:
        extra_specs = [pl.BlockSpec((4, DIFF_QK_DIM), lambda b, u: (0, 0)),
                       pl.BlockSpec((ups, 1, UNIT), lambda b, u: (u, 0, 0))]
    stat = pltpu.VMEM((ups, 2, 1, tq), F32)
    acc = pltpu.VMEM((ups, 2, UNIT + ONES_ROWS, tq), F32)
    score = pltpu.VMEM((ups, 2, tq, tq), F32)
    prob = pltpu.VMEM((ups, 2, tq, tq), BF16)
    return pl.pallas_call(
        functools.partial(_flash_kernel, fox=fox, tq=tq, seq=seq, lambda_init=lambda_init),
        grid=(bsz, N_UNITS // ups),
        in_specs=[q_spec, k_spec, q_spec] + extra_specs,
        out_specs=pl.BlockSpec((None, lp, ups * UNIT), lambda b, u: (b, 0, u)),
        out_shape=jax.ShapeDtypeStruct((bsz, lp, N_UNITS * UNIT), BF16),
        scratch_shapes=[stat, acc, acc, pltpu.VMEM((ups, 2, UNIT, tq), BF16),
                        score, score, prob, prob, stat, stat, stat, stat],
        compiler_params=pltpu.CompilerParams(
            dimension_semantics=("parallel", "parallel"), vmem_limit_bytes=VMEM_LIMIT),
        name="fox_flash" if fox else "diff_flash",
    )(qt, k, vt, *extra)


def _out_kernel(*refs, final, split_meta):
    oa_ref, ob_ref, z_ref, w_ref, h_ref = refs[:5]
    meta_ref = refs[5] if split_meta else None
    g_ref = refs[-2] if final else None
    o_ref = refs[-1]
    half = oa_ref.shape[-1]
    mixed_a = (oa_ref[...].astype(F32) * z_ref[:, 0:half]).astype(BF16)
    mixed_b = (ob_ref[...].astype(F32) * z_ref[:, half:2 * half]).astype(BF16)
    h = (_hidden_tile(h_ref, meta_ref, pl.program_id(1) == pl.num_programs(1) - 1)
         + jnp.dot(mixed_a, w_ref[0:half, :], preferred_element_type=F32)
         + jnp.dot(mixed_b, w_ref[half:2 * half, :], preferred_element_type=F32))
    if final:
        h = h * lax.rsqrt(jnp.mean(h * h, axis=-1, keepdims=True) + NORM_EPS) * g_ref[...]
    o_ref[...] = h


def _out_project(o_a, o_b, zg, w_out, h, meta_tile, final_g, *, tl, rows):
    bsz, _, dm = h.shape
    half = o_a.shape[-1]
    final = final_g is not None
    split_meta = meta_tile is not None
    nt = rows // tl
    row_spec = lambda width: pl.BlockSpec((None, tl, width), lambda b, i: (b, i, 0))
    const = lambda shape: pl.BlockSpec(shape, lambda b, i: (0,) * len(shape))
    in_specs = [row_spec(half), row_spec(half), row_spec(2 * half), const((2 * half, dm))]
    args = [o_a, o_b, zg, w_out, h]
    if split_meta:
        in_specs += [pl.BlockSpec((None, tl, dm), lambda b, i: (b, jnp.minimum(i, nt - 2), 0)),
                     const((tl, dm))]
        args.append(meta_tile)
    else:
        in_specs.append(row_spec(dm))
    if final:
        in_specs.append(const((1, dm)))
        args.append(final_g)
    return pl.pallas_call(
        functools.partial(_out_kernel, final=final, split_meta=split_meta),
        grid=(bsz, nt),
        in_specs=in_specs,
        out_specs=row_spec(dm),
        out_shape=jax.ShapeDtypeStruct((bsz, rows, dm), F32),
        compiler_params=pltpu.CompilerParams(
            dimension_semantics=("parallel", "parallel"), vmem_limit_bytes=VMEM_LIMIT),
        name="out_final" if final else "out_proj",
    )(*args)


def _rope_tables(seq, lp):
    half = DIFF_QK_DIM // 2
    pos = np.zeros((lp,), np.float64)
    pos[:seq] = N_META + np.arange(seq)
    pos[seq:seq + N_META] = np.arange(N_META)
    inv = ROPE_THETA ** (-np.arange(half, dtype=np.float64) / half)
    ang = pos[:, None] * np.tile(inv, UNIT // half)[None, :]
    sign = np.where(np.arange(UNIT) < UNIT // 2, -1.0, 1.0)[None, :]
    cos, sin = np.cos(ang), np.sin(ang) * sign
    q_scale = DIFF_QK_DIM ** -0.5 * LOG2E
    return tuple(jnp.asarray(t, F32) for t in (cos * q_scale, sin * q_scale, cos, sin))


def _unit_lane_order(w_qk):
    dm, width = w_qk.shape
    half = DIFF_QK_DIM // 2
    w5 = w_qk.reshape(dm, width // (4 * half), 2, 2, half)
    return w5.transpose(0, 1, 3, 2, 4).reshape(dm, width)


def _projection_weight(w_layer):
    width = N_UNITS * UNIT
    w_layer = w_layer.astype(BF16)
    sec = lambda k: w_layer[:, k * width:(k + 1) * width]
    logits = jnp.pad(w_layer[:, 8 * width:], ((0, 0), (0, UNIT - FOX_HEADS)))
    return jnp.concatenate(
        [_unit_lane_order(sec(0)), _unit_lane_order(sec(1)), sec(2), sec(4), sec(5), sec(6),
         sec(3), sec(7), logits], axis=1)


def kernel(x, meta_tokens, norm_g, w_in, b_forget, lam_q1, lam_k1, lam_q2, lam_k2, subln_g,
           w_out, final_g):
    bsz, seq, dm = x.shape
    depth = w_in.shape[0]
    tl = SEQ_TILE
    lp = seq + tl
    width = N_UNITS * UNIT
    assert seq % tl == 0 and N_META <= UNIT and w_in.shape[2] == 8 * width + FOX_HEADS

    h, meta_tile = x, jnp.pad(meta_tokens.astype(x.dtype), ((0, tl - N_META), (0, 0)))

    tabs = _rope_tables(seq, lp)
    tri = jnp.asarray(np.tril(np.ones((tl, tl), np.float32)), BF16)

    out = None
    for layer in range(depth):
        lambda_init = 0.8 - 0.6 * math.exp(-0.3 * layer)
        w = _projection_weight(w_in[layer])
        bf = jnp.pad(b_forget[layer].astype(F32), (0, UNIT - FOX_HEADS)).reshape(1, UNIT)
        (qdt, kd, vdt, qft, kf, vft, zg, cum) = _project(
            h, meta_tile, norm_g[layer].reshape(1, dm), w, bf, tabs, tri, tl=tl, lp=lp)

        lam = jnp.stack([lam_q1[layer], lam_k1[layer], lam_q2[layer], lam_k2[layer]]).astype(F32)
        o_a = _flash(qdt, kd, vdt, (lam, subln_g[layer].reshape(N_UNITS, 1, UNIT)),
                     fox=False, tq=tl, seq=seq, lambda_init=lambda_init)
        o_b = _flash(qft, kf, vft, (cum.reshape(bsz, N_UNITS, 2, lp),),
                     fox=True, tq=tl, seq=seq, lambda_init=lambda_init)

        w_o = w_out[layer].astype(BF16)
        if layer + 1 < depth:
            h = _out_project(o_a, o_b, zg, w_o, h, meta_tile, None, tl=tl, rows=lp)
            meta_tile = None
        else:
            out = _out_project(o_a, o_b, zg, w_o, h, None, final_g.reshape(1, dm),
                               tl=tl, rows=seq)
    return out
```

```python
import functools
import math

import numpy as np
import jax
import jax.numpy as jnp
from jax import lax
from jax.experimental import pallas as pl
from jax.experimental.pallas import tpu as pltpu

N_META = 16
DIFF_QK_DIM = 64
FOX_HEAD_DIM = 64
FOX_HEADS = 8
UNIT = 128
N_UNITS = 4
UNITS_PER_STEP = 2
ROPE_THETA = 10000.0
NORM_EPS = 1e-6
LOG2E = 1.0 / math.log(2.0)
NEG = -1e30
ONES_ROWS = 16

SEQ_TILE = 512
VMEM_LIMIT = 56 * 1024 * 1024

F32 = jnp.float32
BF16 = jnp.bfloat16


def _hidden_tile(h_ref, meta_ref, is_meta_tile):
    if meta_ref is None:
        return h_ref[...]
    return jnp.where(is_meta_tile, meta_ref[...], h_ref[...])


def _proj_kernel(*refs, tl, fox_scale, split_meta):
    h_ref, meta_ref = (refs[0], refs[1]) if split_meta else (refs[0], None)
    (g_ref, w_ref, bf_ref, cosq_ref, sinq_ref, cosk_ref, sink_ref, tri_ref,
     qdt_ref, kd_ref, vdt_ref, qft_ref, kf_ref, vft_ref, zg_ref, cum_ref,
     carry_ref) = refs[2 if split_meta else 1:]
    step = pl.program_id(1)

    h = _hidden_tile(h_ref, meta_ref, step == 0)
    u = h * lax.rsqrt(jnp.mean(h * h, axis=-1, keepdims=True) + NORM_EPS) * g_ref[...]
    ub = u.astype(BF16)

    def proj(lo, width):
        return jnp.dot(ub, w_ref[:, lo:lo + width], preferred_element_type=F32)

    def rope(t, cos_ref, sin_ref):
        return t * cos_ref[...] + pltpu.roll(t, 64, 1) * sin_ref[...]

    width = N_UNITS * UNIT
    dq = proj(0 * width, width)
    for un in range(N_UNITS):
        qdt_ref[un] = rope(dq[:, un * UNIT:(un + 1) * UNIT], cosq_ref, sinq_ref).T.astype(BF16)
    dk = proj(1 * width, width)
    for un in range(N_UNITS):
        kd_ref[un] = rope(dk[:, un * UNIT:(un + 1) * UNIT], cosk_ref, sink_ref).astype(BF16)
    dv = proj(2 * width, width)
    for un in range(N_UNITS):
        vdt_ref[un] = dv[:, un * UNIT:(un + 1) * UNIT].T.astype(BF16)
    fq = proj(3 * width, width) * fox_scale
    for un in range(N_UNITS):
        qft_ref[un] = fq[:, un * UNIT:(un + 1) * UNIT].T.astype(BF16)
    fk = proj(4 * width, width)
    for un in range(N_UNITS):
        kf_ref[un] = fk[:, un * UNIT:(un + 1) * UNIT].astype(BF16)
    fv = proj(5 * width, width)
    for un in range(N_UNITS):
        vft_ref[un] = fv[:, un * UNIT:(un + 1) * UNIT].T.astype(BF16)

    z = proj(6 * width, 2 * width)
    zg_ref[...] = z * (1.0 / (1.0 + jnp.exp(-z)))

    fl = proj(8 * width, UNIT) + bf_ref[...]
    ls = (jnp.minimum(fl, 0.0) - jnp.log1p(jnp.exp(-jnp.abs(fl)))) * LOG2E
    row = lax.broadcasted_iota(jnp.int32, (tl, UNIT), 0)
    ls = jnp.where(row >= jnp.where(step == 0, N_META, tl), 0.0, ls)

    @pl.when(step == 0)
    def _():
        carry_ref[...] = jnp.zeros_like(carry_ref)

    hi = ls.astype(BF16)
    r1 = ls - hi.astype(F32)
    mid = r1.astype(BF16)
    lo = (r1 - mid.astype(F32)).astype(BF16)
    tri = tri_ref[...]
    cum = (jnp.dot(tri, hi, preferred_element_type=F32)
           + jnp.dot(tri, mid, preferred_element_type=F32)
           + jnp.dot(tri, lo, preferred_element_type=F32)) + carry_ref[...]
    carry_ref[...] = cum[tl - 1:tl, :]
    cum_ref[...] = cum.T[0:FOX_HEADS, :]


def _project(h, meta_tile, norm_g, w, b_forget, tabs, tri, *, tl, lp):
    bsz, _, dm = h.shape
    nl = lp // tl
    pw = w.shape[1]
    split_meta = meta_tile is not None

    def tok(b, i):
        return (i + nl - 1) % nl

    row_spec = lambda width: pl.BlockSpec((None, tl, width), lambda b, i: (b, tok(b, i), 0))
    unit_rows = pl.BlockSpec((None, N_UNITS, tl, UNIT), lambda b, i: (b, 0, tok(b, i), 0))
    unit_cols = pl.BlockSpec((None, N_UNITS, UNIT, tl), lambda b, i: (b, 0, 0, tok(b, i)))
    tab_spec = pl.BlockSpec((tl, UNIT), lambda b, i: (tok(b, i), 0))
    const = lambda shape: pl.BlockSpec(shape, lambda b, i: (0,) * len(shape))

    unit_t = jax.ShapeDtypeStruct((bsz, N_UNITS, UNIT, lp), BF16)
    unit_n = jax.ShapeDtypeStruct((bsz, N_UNITS, lp, UNIT), BF16)
    if split_meta:
        h_specs = [pl.BlockSpec((None, tl, dm), lambda b, i: (b, jnp.minimum(tok(b, i), nl - 2), 0)),
                   const((tl, dm))]
        h_args = [h, meta_tile]
    else:
        h_specs, h_args = [row_spec(dm)], [h]
    return pl.pallas_call(
        functools.partial(_proj_kernel, tl=tl, fox_scale=FOX_HEAD_DIM ** -0.5 * LOG2E,
                          split_meta=split_meta),
        grid=(bsz, nl),
        in_specs=h_specs + [const((1, dm)), const((dm, pw)), const((1, UNIT)),
                            tab_spec, tab_spec, tab_spec, tab_spec, const((tl, tl))],
        out_specs=[unit_cols, unit_rows, unit_cols, unit_cols, unit_rows, unit_cols,
                   row_spec(2 * N_UNITS * UNIT),
                   pl.BlockSpec((None, FOX_HEADS, tl), lambda b, i: (b, 0, tok(b, i)))],
        out_shape=[unit_t, unit_n, unit_t, unit_t, unit_n, unit_t,
                   jax.ShapeDtypeStruct((bsz, lp, 2 * N_UNITS * UNIT), F32),
                   jax.ShapeDtypeStruct((bsz, FOX_HEADS, lp), F32)],
        scratch_shapes=[pltpu.VMEM((1, UNIT), F32)],
        compiler_params=pltpu.CompilerParams(
            dimension_semantics=("arbitrary", "arbitrary"), vmem_limit_bytes=VMEM_LIMIT),
        name="proj",
    )(*h_args, norm_g, w, b_forget, *tabs, tri)


def _flash_kernel(*refs, fox, tq, seq, lambda_init):
    if fox:
        qt_ref, k_ref, vt_ref, cum_ref, o_ref = refs[:5]
        n_in = 5
    else:
        qt_ref, k_ref, vt_ref, lam_ref, g_ref, o_ref = refs[:6]
        n_in = 6
    m_sc, acc_sc, acc0_sc, q_sc = refs[n_in:n_in + 4]
    s_buf, p_buf, alpha_buf, mblk_buf = (
        refs[n_in + 4 + 2 * n:n_in + 6 + 2 * n] for n in range(4))
    units = range(UNITS_PER_STEP)
    tk = tq
    nq = seq // tq
    n_steps = nq * (nq + 1) // 2

    def tile_start(t):
        return pl.multiple_of(t * tq, tq)

    def load_queries(un, i):
        feat = lax.broadcasted_iota(jnp.int32, (UNIT, tq), 0)
        second = (feat >= FOX_HEAD_DIM) if fox else ((feat // 32) % 2 == 1)
        qt = qt_ref[un, :, pl.ds(tile_start(i), tq)].astype(F32)
        q_sc[un, 0] = jnp.where(second, 0.0, qt).astype(BF16)
        q_sc[un, 1] = jnp.where(second, qt, 0.0).astype(BF16)

    def scores(un, k_off, k_len, c):
        s = jnp.dot(k_ref[un, pl.ds(k_off, k_len), :], q_sc[un, c], preferred_element_type=F32)
        if fox:
            ck = jnp.broadcast_to(cum_ref[un, c:c + 1, pl.ds(k_off, k_len)], (UNIT, k_len)).T
            s = jnp.concatenate(
                [s[:, b * UNIT:(b + 1) * UNIT] - ck for b in range(tq // UNIT)], axis=1)
        return s

    def issue_scores(un, j, par, cs=(0, 1)):
        for c in cs:
            s = scores(un, tile_start(j), tk, c)
            s_buf[par][un, c] = s
            mblk_buf[par][un, c] = jnp.max(s, axis=0, keepdims=True)

    def softmax_core(un, i, c, s, m_blk, m_old):
        if fox:
            cq = cum_ref[un, c:c + 1, pl.ds(tile_start(i), tq)]
            m_new = jnp.maximum(m_old, m_blk + cq)
            shift = m_new - cq
        else:
            m_new = jnp.maximum(m_old, m_blk)
            shift = m_new
        return m_new, jnp.exp2(m_old - m_new), jnp.exp2(s - shift)

    def softmax(un, i, par, diagonal=False, cs=(0, 1)):
        for c in cs:
            s = s_buf[par][un, c]
            if diagonal:
                s = jnp.where(lax.broadcasted_iota(jnp.int32, (tk, tq), 0)
                              <= lax.broadcasted_iota(jnp.int32, (tk, tq), 1), s, NEG)
                m_blk = jnp.max(s, axis=0, keepdims=True)
            else:
                m_blk = mblk_buf[par][un, c]
            m_new, alpha, p = softmax_core(un, i, c, s, m_blk, m_sc[un, c])
            alpha_buf[par][un, c] = alpha
            p_buf[par][un, c] = p.astype(BF16)
            m_sc[un, c] = m_new

    def pv_update(un, j, par, cs=(0, 1)):
        vtt = jnp.concatenate(
            [vt_ref[un, :, pl.ds(tile_start(j), tk)], jnp.ones((ONES_ROWS, tk), BF16)], axis=0)
        for c in cs:
            acc_sc[un, c] = alpha_buf[par][un, c] * acc_sc[un, c] + jnp.dot(
                vtt, p_buf[par][un, c], preferred_element_type=F32)

    def meta_scores(un, causal):
        key = lax.broadcasted_iota(jnp.int32, (UNIT, tq), 0)
        mask = key < N_META
        if causal:
            mask = jnp.logical_and(mask, key <= lax.broadcasted_iota(jnp.int32, (UNIT, tq), 1))
        return [jnp.where(mask, scores(un, seq, UNIT, c), NEG) for c in range(2)]

    def meta_softmax(un, i, s_pair):
        p_pair = []
        for c, s in enumerate(s_pair):
            m_new, _, p = softmax_core(un, i, c, s, jnp.max(s, axis=0, keepdims=True),
                                       jnp.full((1, tq), NEG, F32))
            m_sc[un, c] = m_new
            p_pair.append(p.astype(BF16))
        return p_pair

    def meta_values(un, p_pair):
        vtt = vt_ref[un, :, pl.ds(seq, UNIT)]
        for c, p in enumerate(p_pair):
            acc0_sc[un, c, 0:UNIT] = jnp.dot(vtt, p, preferred_element_type=F32)
            denom = jnp.sum(p.astype(F32), axis=0, keepdims=True)
            acc0_sc[un, c, UNIT:UNIT + ONES_ROWS] = jnp.broadcast_to(denom, (ONES_ROWS, tq))

    def start_row(un, i, causal):
        meta_values(un, meta_softmax(un, i, meta_scores(un, causal)))

    def normalized(un, c):
        acc = acc_sc[un, c]
        return acc[:UNIT] * (1.0 / acc[UNIT:UNIT + 1])

    def finalize(un, i):
        o1 = normalized(un, 0)
        o2 = normalized(un, 1)
        if fox:
            o = jnp.concatenate([o1[:FOX_HEAD_DIM], o2[FOX_HEAD_DIM:]], axis=0).T
        else:
            lam = (jnp.exp(jnp.sum(lam_ref[0:1, :] * lam_ref[1:2, :], axis=-1, keepdims=True))
                   - jnp.exp(jnp.sum(lam_ref[2:3, :] * lam_ref[3:4, :], axis=-1, keepdims=True))
                   + lambda_init)
            o = (o1 - lam * o2).T
            o = o * lax.rsqrt(jnp.mean(o * o, axis=-1, keepdims=True) + NORM_EPS) * g_ref[un]
            o = o * (1.0 - lambda_init)
        o_ref[pl.ds(tile_start(i), tq), un * UNIT:(un + 1) * UNIT] = o.astype(BF16)

    def diag_softmax(i, par):
        for un in units:
            for c in range(2):
                softmax(un, i, par, diagonal=True, cs=(c,))
                pv_update(un, jnp.maximum(i - 1, 0), 1 - par, cs=(c,))

    def next_row_scores(i, par):
        for un in units:
            load_queries(un, i + 1)
        s_meta = [meta_scores(un, causal=False) for un in units]
        for un in units:
            issue_scores(un, 0, 1 - par)
        return s_meta

    def next_row_start(i, s_meta):
        p_meta = [meta_softmax(un, i + 1, s_meta[un]) for un in units]
        for un in units:
            meta_values(un, p_meta[un])

    def step(i, j, par):
        @pl.when(jnp.logical_and(j > 0, j < i))
        def _():
            for un in units:
                for c in range(2):
                    softmax(un, i, par, cs=(c,))
                    pv_update(un, j - 1, 1 - par, cs=(c,))
                    issue_scores(un, j + 1, 1 - par, cs=(c,))

        @pl.when(jnp.logical_and(j == 0, i > 0))
        def _():
            for un in units:
                for c in range(2):
                    softmax(un, i, par, cs=(c,))
                    pv_update(un, i - 1, 1 - par, cs=(c,))
                    issue_scores(un, 1, 1 - par, cs=(c,))
            for un in units:
                finalize(un, i - 1)
                acc_sc[un] = acc0_sc[un]

        @pl.when(j == i)
        def _():
            s_meta = next_row_scores(i, par)
            diag_softmax(i, par)
            next_row_start(i, s_meta)

        wrap = j == i
        return jnp.where(wrap, i + 1, i), jnp.where(wrap, 0, j + 1)

    for un in units:
        load_queries(un, 0)
        start_row(un, 0, causal=False)
        acc_sc[un] = acc0_sc[un]
        issue_scores(un, 0, 0)
    alpha_buf[1][...] = jnp.ones_like(alpha_buf[1])
    p_buf[1][...] = jnp.zeros_like(p_buf[1])

    def body(_, ij):
        i, j = step(ij[0], ij[1], 0)
        return step(i, j, 1)

    i, j = lax.fori_loop(0, (n_steps - 2) // 2, body, (jnp.int32(0), jnp.int32(0)))
    step(i, j, 0)
    diag_softmax(nq - 1, 1)
    for un in units:
        pv_update(un, nq - 1, 1)
        finalize(un, nq - 1)

    for un in units:
        load_queries(un, nq)
        start_row(un, nq, causal=True)
        acc_sc[un] = acc0_sc[un]
        finalize(un, nq)


def _flash(qt, k, vt, extra, *, fox, tq, seq, lambda_init):
    bsz, _, lp, _ = k.shape
    nq = seq // tq
    ups = UNITS_PER_STEP
    assert lp == seq + tq and (nq * (nq + 1) // 2) % 2 == 0 and N_UNITS % ups == 0
    q_spec = pl.BlockSpec((None, ups, UNIT, lp), lambda b, u: (b, u, 0, 0))
    k_spec = pl.BlockSpec((None, ups, lp, UNIT), lambda b, u: (b, u, 0, 0))
    if fox:
        extra_specs = [pl.BlockSpec((None, ups, 2, lp), lambda b, u: (b, u, 0, 0))]
    else:
        extra_specs = [pl.BlockSpec((4, DIFF_QK_DIM), lambda b, u: (0, 0)),
                       pl.BlockSpec((ups, 1, UNIT), lambda b, u: (u, 0, 0))]
    stat = pltpu.VMEM((ups, 2, 1, tq), F32)
    acc = pltpu.VMEM((ups, 2, UNIT + ONES_ROWS, tq), F32)
    score = pltpu.VMEM((ups, 2, tq, tq), F32)
    prob = pltpu.VMEM((ups, 2, tq, tq), BF16)
    return pl.pallas_call(
        functools.partial(_flash_kernel, fox=fox, tq=tq, seq=seq, lambda_init=lambda_init),
        grid=(bsz, N_UNITS // ups),
        in_specs=[q_spec, k_spec, q_spec] + extra_specs,
        out_specs=pl.BlockSpec((None, lp, ups * UNIT), lambda b, u: (b, 0, u)),
        out_shape=jax.ShapeDtypeStruct((bsz, lp, N_UNITS * UNIT), BF16),
        scratch_shapes=[stat, acc, acc, pltpu.VMEM((ups, 2, UNIT, tq), BF16),
                        score, score, prob, prob, stat, stat, stat, stat],
        compiler_params=pltpu.CompilerParams(
            dimension_semantics=("parallel", "parallel"), vmem_limit_bytes=VMEM_LIMIT),
        name="fox_flash" if fox else "diff_flash",
    )(qt, k, vt, *extra)


def _out_kernel(*refs, final, split_meta):
    oa_ref, ob_ref, z_ref, w_ref, h_ref = refs[:5]
    meta_ref = refs[5] if split_meta else None
    g_ref = refs[-2] if final else None
    o_ref = refs[-1]
    half = oa_ref.shape[-1]
    mixed_a = (oa_ref[...].astype(F32) * z_ref[:, 0:half]).astype(BF16)
    mixed_b = (ob_ref[...].astype(F32) * z_ref[:, half:2 * half]).astype(BF16)
    h = (_hidden_tile(h_ref, meta_ref, pl.program_id(1) == pl.num_programs(1) - 1)
         + jnp.dot(mixed_a, w_ref[0:half, :], preferred_element_type=F32)
         + jnp.dot(mixed_b, w_ref[half:2 * half, :], preferred_element_type=F32))
    if final:
        h = h * lax.rsqrt(jnp.mean(h * h, axis=-1, keepdims=True) + NORM_EPS) * g_ref[...]
    o_ref[...] = h


def _out_project(o_a, o_b, zg, w_out, h, meta_tile, final_g, *, tl, rows):
    bsz, _, dm = h.shape
    half = o_a.shape[-1]
    final = final_g is not None
    split_meta = meta_tile is not None
    nt = rows // tl
    row_spec = lambda width: pl.BlockSpec((None, tl, width), lambda b, i: (b, i, 0))
    const = lambda shape: pl.BlockSpec(shape, lambda b, i: (0,) * len(shape))
    in_specs = [row_spec(half), row_spec(half), row_spec(2 * half), const((2 * half, dm))]
    args = [o_a, o_b, zg, w_out, h]
    if split_meta:
        in_specs += [pl.BlockSpec((None, tl, dm), lambda b, i: (b, jnp.minimum(i, nt - 2), 0)),
                     const((tl, dm))]
        args.append(meta_tile)
    else:
        in_specs.append(row_spec(dm))
    if final:
        in_specs.append(const((1, dm)))
        args.append(final_g)
    return pl.pallas_call(
        functools.partial(_out_kernel, final=final, split_meta=split_meta),
        grid=(bsz, nt),
        in_specs=in_specs,
        out_specs=row_spec(dm),
        out_shape=jax.ShapeDtypeStruct((bsz, rows, dm), F32),
        compiler_params=pltpu.CompilerParams(
            dimension_semantics=("parallel", "parallel"), vmem_limit_bytes=VMEM_LIMIT),
        name="out_final" if final else "out_proj",
    )(*args)


def _rope_tables(seq, lp):
    half = DIFF_QK_DIM // 2
    pos = np.zeros((lp,), np.float64)
    pos[:seq] = N_META + np.arange(seq)
    pos[seq:seq + N_META] = np.arange(N_META)
    inv = ROPE_THETA ** (-np.arange(half, dtype=np.float64) / half)
    ang = pos[:, None] * np.tile(inv, UNIT // half)[None, :]
    sign = np.where(np.arange(UNIT) < UNIT // 2, -1.0, 1.0)[None, :]
    cos, sin = np.cos(ang), np.sin(ang) * sign
    q_scale = DIFF_QK_DIM ** -0.5 * LOG2E
    return tuple(jnp.asarray(t, F32) for t in (cos * q_scale, sin * q_scale, cos, sin))


def _unit_lane_order(w_qk):
    dm, width = w_qk.shape
    half = DIFF_QK_DIM // 2
    w5 = w_qk.reshape(dm, width // (4 * half), 2, 2, half)
    return w5.transpose(0, 1, 3, 2, 4).reshape(dm, width)


def _projection_weight(w_layer):
    width = N_UNITS * UNIT
    w_layer = w_layer.astype(BF16)
    sec = lambda k: w_layer[:, k * width:(k + 1) * width]
    logits = jnp.pad(w_layer[:, 8 * width:], ((0, 0), (0, UNIT - FOX_HEADS)))
    return jnp.concatenate(
        [_unit_lane_order(sec(0)), _unit_lane_order(sec(1)), sec(2), sec(4), sec(5), sec(6),
         sec(3), sec(7), logits], axis=1)


def kernel(x, meta_tokens, norm_g, w_in, b_forget, lam_q1, lam_k1, lam_q2, lam_k2, subln_g,
           w_out, final_g):
    bsz, seq, dm = x.shape
    depth = w_in.shape[0]
    tl = SEQ_TILE
    lp = seq + tl
    width = N_UNITS * UNIT
    assert seq % tl == 0 and N_META <= UNIT and w_in.shape[2] == 8 * width + FOX_HEADS

    h, meta_tile = x, jnp.pad(meta_tokens.astype(x.dtype), ((0, tl - N_META), (0, 0)))

    tabs = _rope_tables(seq, lp)
    tri = jnp.asarray(np.tril(np.ones((tl, tl), np.float32)), BF16)

    out = None
    for layer in range(depth):
        lambda_init = 0.8 - 0.6 * math.exp(-0.3 * layer)
        w = _projection_weight(w_in[layer])
        bf = jnp.pad(b_forget[layer].astype(F32), (0, UNIT - FOX_HEADS)).reshape(1, UNIT)
        (qdt, kd, vdt, qft, kf, vft, zg, cum) = _project(
            h, meta_tile, norm_g[layer].reshape(1, dm), w, bf, tabs, tri, tl=tl, lp=lp)

        lam = jnp.stack([lam_q1[layer], lam_k1[layer], lam_q2[layer], lam_k2[layer]]).astype(F32)
        o_a = _flash(qdt, kd, vdt, (lam, subln_g[layer].reshape(N_UNITS, 1, UNIT)),
                     fox=False, tq=tl, seq=seq, lambda_init=lambda_init)
        o_b = _flash(qft, kf, vft, (cum.reshape(bsz, N_UNITS, 2, lp),),
                     fox=True, tq=tl, seq=seq, lambda_init=lambda_init)

        w_o = w_out[layer].astype(BF16)
        if layer + 1 < depth:
            h = _out_project(o_a, o_b, zg, w_o, h, meta_tile, None, tl=tl, rows=lp)
            meta_tile = None
        else:
            out = _out_project(o_a, o_b, zg, w_o, h, None, final_g.reshape(1, dm),
                               tl=tl, rows=seq)
    return out
```

```python
import functools
import math

import numpy as np
import jax
import jax.numpy as jnp
from jax import lax
from jax.experimental import pallas as pl
from jax.experimental.pallas import tpu as pltpu

N_META = 16
DIFF_QK_DIM = 64
FOX_HEAD_DIM = 64
FOX_HEADS = 8
UNIT = 128
N_UNITS = 4
UNITS_PER_STEP = 2
ROPE_THETA = 10000.0
NORM_EPS = 1e-6
LOG2E = 1.0 / math.log(2.0)
NEG = -1e30
ONES_ROWS = 16

SEQ_TILE = 512
VMEM_LIMIT = 56 * 1024 * 1024

F32 = jnp.float32
BF16 = jnp.bfloat16


def _hidden_tile(h_ref, meta_ref, is_meta_tile):
    if meta_ref is None:
        return h_ref[...]
    return jnp.where(is_meta_tile, meta_ref[...], h_ref[...])


def _proj_kernel(*refs, tl, fox_scale, split_meta):
    h_ref, meta_ref = (refs[0], refs[1]) if split_meta else (refs[0], None)
    (g_ref, w_ref, bf_ref, cosq_ref, sinq_ref, cosk_ref, sink_ref, tri_ref,
     qdt_ref, kd_ref, vdt_ref, qft_ref, kf_ref, vft_ref, zg_ref, cum_ref,
     carry_ref) = refs[2 if split_meta else 1:]
    step = pl.program_id(1)

    h = _hidden_tile(h_ref, meta_ref, step == 0)
    u = h * lax.rsqrt(jnp.mean(h * h, axis=-1, keepdims=True) + NORM_EPS) * g_ref[...]
    ub = u.astype(BF16)

    def proj(lo, width):
        return jnp.dot(ub, w_ref[:, lo:lo + width], preferred_element_type=F32)

    def rope(t, cos_ref, sin_ref):
        return t * cos_ref[...] + pltpu.roll(t, 64, 1) * sin_ref[...]

    width = N_UNITS * UNIT
    dq = proj(0 * width, width)
    for un in range(N_UNITS):
        qdt_ref[un] = rope(dq[:, un * UNIT:(un + 1) * UNIT], cosq_ref, sinq_ref).T.astype(BF16)
    dk = proj(1 * width, width)
    for un in range(N_UNITS):
        kd_ref[un] = rope(dk[:, un * UNIT:(un + 1) * UNIT], cosk_ref, sink_ref).astype(BF16)
    dv = proj(2 * width, width)
    for un in range(N_UNITS):
        vdt_ref[un] = dv[:, un * UNIT:(un + 1) * UNIT].T.astype(BF16)
    fq = proj(3 * width, width) * fox_scale
    for un in range(N_UNITS):
        qft_ref[un] = fq[:, un * UNIT:(un + 1) * UNIT].T.astype(BF16)
    fk = proj(4 * width, width)
    for un in range(N_UNITS):
        kf_ref[un] = fk[:, un * UNIT:(un + 1) * UNIT].astype(BF16)
    fv = proj(5 * width, width)
    for un in range(N_UNITS):
        vft_ref[un] = fv[:, un * UNIT:(un + 1) * UNIT].T.astype(BF16)

    z = proj(6 * width, 2 * width)
    zg_ref[...] = z * (1.0 / (1.0 + jnp.exp(-z)))

    fl = proj(8 * width, UNIT) + bf_ref[...]
    ls = (jnp.minimum(fl, 0.0) - jnp.log1p(jnp.exp(-jnp.abs(fl)))) * LOG2E
    row = lax.broadcasted_iota(jnp.int32, (tl, UNIT), 0)
    ls = jnp.where(row >= jnp.where(step == 0, N_META, tl), 0.0, ls)

    @pl.when(step == 0)
    def _():
        carry_ref[...] = jnp.zeros_like(carry_ref)

    hi = ls.astype(BF16)
    r1 = ls - hi.astype(F32)
    mid = r1.astype(BF16)
    lo = (r1 - mid.astype(F32)).astype(BF16)
    tri = tri_ref[...]
    cum = (jnp.dot(tri, hi, preferred_element_type=F32)
           + jnp.dot(tri, mid, preferred_element_type=F32)
           + jnp.dot(tri, lo, preferred_element_type=F32)) + carry_ref[...]
    carry_ref[...] = cum[tl - 1:tl, :]
    cum_ref[...] = cum.T[0:FOX_HEADS, :]


def _project(h, meta_tile, norm_g, w, b_forget, tabs, tri, *, tl, lp):
    bsz, _, dm = h.shape
    nl = lp // tl
    pw = w.shape[1]
    split_meta = meta_tile is not None

    def tok(b, i):
        return (i + nl - 1) % nl

    row_spec = lambda width: pl.BlockSpec((None, tl, width), lambda b, i: (b, tok(b, i), 0))
    unit_rows = pl.BlockSpec((None, N_UNITS, tl, UNIT), lambda b, i: (b, 0, tok(b, i), 0))
    unit_cols = pl.BlockSpec((None, N_UNITS, UNIT, tl), lambda b, i: (b, 0, 0, tok(b, i)))
    tab_spec = pl.BlockSpec((tl, UNIT), lambda b, i: (tok(b, i), 0))
    const = lambda shape: pl.BlockSpec(shape, lambda b, i: (0,) * len(shape))

    unit_t = jax.ShapeDtypeStruct((bsz, N_UNITS, UNIT, lp), BF16)
    unit_n = jax.ShapeDtypeStruct((bsz, N_UNITS, lp, UNIT), BF16)
    if split_meta:
        h_specs = [pl.BlockSpec((None, tl, dm), lambda b, i: (b, jnp.minimum(tok(b, i), nl - 2), 0)),
                   const((tl, dm))]
        h_args = [h, meta_tile]
    else:
        h_specs, h_args = [row_spec(dm)], [h]
    return pl.pallas_call(
        functools.partial(_proj_kernel, tl=tl, fox_scale=FOX_HEAD_DIM ** -0.5 * LOG2E,
                          split_meta=split_meta),
        grid=(bsz, nl),
        in_specs=h_specs + [const((1, dm)), const((dm, pw)), const((1, UNIT)),
                            tab_spec, tab_spec, tab_spec, tab_spec, const((tl, tl))],
        out_specs=[unit_cols, unit_rows, unit_cols, unit_cols, unit_rows, unit_cols,
                   row_spec(2 * N_UNITS * UNIT),
                   pl.BlockSpec((None, FOX_HEADS, tl), lambda b, i: (b, 0, tok(b, i)))],
        out_shape=[unit_t, unit_n, unit_t, unit_t, unit_n, unit_t,
                   jax.ShapeDtypeStruct((bsz, lp, 2 * N_UNITS * UNIT), F32),
                   jax.ShapeDtypeStruct((bsz, FOX_HEADS, lp), F32)],
        scratch_shapes=[pltpu.VMEM((1, UNIT), F32)],
        compiler_params=pltpu.CompilerParams(
            dimension_semantics=("arbitrary", "arbitrary"), vmem_limit_bytes=VMEM_LIMIT),
        name="proj",
    )(*h_args, norm_g, w, b_forget, *tabs, tri)


def _flash_kernel(*refs, fox, tq, seq, lambda_init):
    if fox:
        qt_ref, k_ref, vt_ref, cum_ref, o_ref = refs[:5]
        n_in = 5
    else:
        qt_ref, k_ref, vt_ref, lam_ref, g_ref, o_ref = refs[:6]
        n_in = 6
    m_sc, acc_sc, acc0_sc, q_sc = refs[n_in:n_in + 4]
    s_buf, p_buf, alpha_buf, mblk_buf = (
        refs[n_in + 4 + 2 * n:n_in + 6 + 2 * n] for n in range(4))
    units = range(UNITS_PER_STEP)
    tk = tq
    nq = seq // tq
    n_steps = nq * (nq + 1) // 2

    def tile_start(t):
        return pl.multiple_of(t * tq, tq)

    def load_queries(un, i):
        feat = lax.broadcasted_iota(jnp.int32, (UNIT, tq), 0)
        second = (feat >= FOX_HEAD_DIM) if fox else ((feat // 32) % 2 == 1)
        qt = qt_ref[un, :, pl.ds(tile_start(i), tq)].astype(F32)
        q_sc[un, 0] = jnp.where(second, 0.0, qt).astype(BF16)
        q_sc[un, 1] = jnp.where(second, qt, 0.0).astype(BF16)

    all_lanes = slice(0, tq)
    lane_halves = (slice(0, tq // 2), slice(tq // 2, tq))

    def scores(un, k_off, k_len, c, lanes=all_lanes):
        s = jnp.dot(k_ref[un, pl.ds(k_off, k_len), :], q_sc[un, c, :, lanes],
                    preferred_element_type=F32)
        if fox:
            ck = jnp.broadcast_to(cum_ref[un, c:c + 1, pl.ds(k_off, k_len)], (UNIT, k_len)).T
            s = jnp.concatenate(
                [s[:, b * UNIT:(b + 1) * UNIT] - ck for b in range(s.shape[1] // UNIT)], axis=1)
        return s

    def issue_scores(un, j, par, cs=(0, 1), lanes=all_lanes):
        for c in cs:
            s = scores(un, tile_start(j), tk, c, lanes)
            s_buf[par][un, c, :, lanes] = s
            mblk_buf[par][un, c, :, lanes] = jnp.max(s, axis=0, keepdims=True)

    def softmax_core(un, i, c, s, m_blk, m_old, lanes=all_lanes):
        if fox:
            cq = cum_ref[un, c:c + 1, pl.ds(tile_start(i) + lanes.start, lanes.stop - lanes.start)]
            m_new = jnp.maximum(m_old, m_blk + cq)
            shift = m_new - cq
        else:
            m_new = jnp.maximum(m_old, m_blk)
            shift = m_new
        return m_new, jnp.exp2(m_old - m_new), jnp.exp2(s - shift)

    def softmax(un, i, par, cs=(0, 1), lanes=all_lanes):
        for c in cs:
            m_new, alpha, p = softmax_core(un, i, c, s_buf[par][un, c, :, lanes],
                                           mblk_buf[par][un, c, :, lanes],
                                           m_sc[un, c, :, lanes], lanes)
            alpha_buf[par][un, c, :, lanes] = alpha
            p_buf[par][un, c, :, lanes] = p.astype(BF16)
            m_sc[un, c, :, lanes] = m_new

    def softmax_diagonal(un, i, par, c):
        half = tq // 2
        below = (lax.broadcasted_iota(jnp.int32, (half, half), 0)
                 <= lax.broadcasted_iota(jnp.int32, (half, half), 1))
        for part, lanes in enumerate(lane_halves):
            s = jnp.where(below, s_buf[par][un, c, lanes, lanes], NEG)
            if part == 1:
                s = jnp.concatenate([s_buf[par][un, c, 0:half, lanes], s], axis=0)
            m_new, alpha, p = softmax_core(un, i, c, s, jnp.max(s, axis=0, keepdims=True),
                                           m_sc[un, c, :, lanes], lanes)
            alpha_buf[par][un, c, :, lanes] = alpha
            m_sc[un, c, :, lanes] = m_new
            if part == 0:
                p_buf[par][un, c, 0:half, lanes] = p.astype(BF16)
                p_buf[par][un, c, half:tk, lanes] = jnp.zeros((tk - half, half), BF16)
            else:
                p_buf[par][un, c, :, lanes] = p.astype(BF16)

    def pv_update(un, j, par, cs=(0, 1), lanes=all_lanes):
        vtt = jnp.concatenate(
            [vt_ref[un, :, pl.ds(tile_start(j), tk)], jnp.ones((ONES_ROWS, tk), BF16)], axis=0)
        for c in cs:
            acc_sc[un, c, :, lanes] = (
                alpha_buf[par][un, c, :, lanes] * acc_sc[un, c, :, lanes]
                + jnp.dot(vtt, p_buf[par][un, c, :, lanes], preferred_element_type=F32))

    def meta_scores(un, causal):
        key = lax.broadcasted_iota(jnp.int32, (UNIT, tq), 0)
        mask = key < N_META
        if causal:
            mask = jnp.logical_and(mask, key <= lax.broadcasted_iota(jnp.int32, (UNIT, tq), 1))
        return [jnp.where(mask, scores(un, seq, UNIT, c), NEG) for c in range(2)]

    def meta_softmax(un, i, s_pair):
        p_pair = []
        for c, s in enumerate(s_pair):
            m_new, _, p = softmax_core(un, i, c, s, jnp.max(s, axis=0, keepdims=True),
                                       jnp.full((1, tq), NEG, F32))
            m_sc[un, c] = m_new
            p_pair.append(p.astype(BF16))
        return p_pair

    def meta_values(un, p_pair):
        vtt = vt_ref[un, :, pl.ds(seq, UNIT)]
        for c, p in enumerate(p_pair):
            acc0_sc[un, c, 0:UNIT] = jnp.dot(vtt, p, preferred_element_type=F32)
            denom = jnp.sum(p.astype(F32), axis=0, keepdims=True)
            acc0_sc[un, c, UNIT:UNIT + ONES_ROWS] = jnp.broadcast_to(denom, (ONES_ROWS, tq))

    def start_row(un, i, causal):
        meta_values(un, meta_softmax(un, i, meta_scores(un, causal)))

    def normalized(un, c):
        acc = acc_sc[un, c]
        return acc[:UNIT] * (1.0 / acc[UNIT:UNIT + 1])

    def finalize(un, i):
        o1 = normalized(un, 0)
        o2 = normalized(un, 1)
        if fox:
            o = jnp.concatenate([o1[:FOX_HEAD_DIM], o2[FOX_HEAD_DIM:]], axis=0).T
        else:
            lam = (jnp.exp(jnp.sum(lam_ref[0:1, :] * lam_ref[1:2, :], axis=-1, keepdims=True))
                   - jnp.exp(jnp.sum(lam_ref[2:3, :] * lam_ref[3:4, :], axis=-1, keepdims=True))
                   + lambda_init)
            o = (o1 - lam * o2).T
            o = o * lax.rsqrt(jnp.mean(o * o, axis=-1, keepdims=True) + NORM_EPS) * g_ref[un]
            o = o * (1.0 - lambda_init)
        o_ref[pl.ds(tile_start(i), tq), un * UNIT:(un + 1) * UNIT] = o.astype(BF16)

    def diag_softmax(i, par):
        for un in units:
            for c in range(2):
                softmax_diagonal(un, i, par, c)
                pv_update(un, jnp.maximum(i - 1, 0), 1 - par, cs=(c,))

    def next_row_scores(i, par):
        for un in units:
            load_queries(un, i + 1)
        s_meta = [meta_scores(un, causal=False) for un in units]
        for un in units:
            issue_scores(un, 0, 1 - par)
        return s_meta

    def next_row_start(i, s_meta):
        p_meta = [meta_softmax(un, i + 1, s_meta[un]) for un in units]
        for un in units:
            meta_values(un, p_meta[un])

    def step(i, j, par):
        @pl.when(jnp.logical_and(j > 0, j < i))
        def _():
            for un in units:
                for c in range(2):
                    for lanes in lane_halves:
                        softmax(un, i, par, cs=(c,), lanes=lanes)
                        pv_update(un, j - 1, 1 - par, cs=(c,), lanes=lanes)
                        issue_scores(un, j + 1, 1 - par, cs=(c,), lanes=lanes)

        @pl.when(jnp.logical_and(j == 0, i > 0))
        def _():
            for un in units:
                for c in range(2):
                    softmax(un, i, par, cs=(c,))
                    pv_update(un, i - 1, 1 - par, cs=(c,))
                    issue_scores(un, 1, 1 - par, cs=(c,))
            for un in units:
                finalize(un, i - 1)
                acc_sc[un] = acc0_sc[un]

        @pl.when(j == i)
        def _():
            s_meta = next_row_scores(i, par)
            diag_softmax(i, par)
            next_row_start(i, s_meta)

        wrap = j == i
        return jnp.where(wrap, i + 1, i), jnp.where(wrap, 0, j + 1)

    for un in units:
        load_queries(un, 0)
        start_row(un, 0, causal=False)
        acc_sc[un] = acc0_sc[un]
        issue_scores(un, 0, 0)
    alpha_buf[1][...] = jnp.ones_like(alpha_buf[1])
    p_buf[1][...] = jnp.zeros_like(p_buf[1])

    def body(_, ij):
        i, j = step(ij[0], ij[1], 0)
        return step(i, j, 1)

    i, j = lax.fori_loop(0, (n_steps - 2) // 2, body, (jnp.int32(0), jnp.int32(0)))
    step(i, j, 0)
    diag_softmax(nq - 1, 1)
    for un in units:
        pv_update(un, nq - 1, 1)
        finalize(un, nq - 1)

    for un in units:
        load_queries(un, nq)
        start_row(un, nq, causal=True)
        acc_sc[un] = acc0_sc[un]
        finalize(un, nq)


def _flash(qt, k, vt, extra, *, fox, tq, seq, lambda_init):
    bsz, _, lp, _ = k.shape
    nq = seq // tq
    ups = UNITS_PER_STEP
    assert lp == seq + tq and (nq * (nq + 1) // 2) % 2 == 0 and N_UNITS % ups == 0
    q_spec = pl.BlockSpec((None, ups, UNIT, lp), lambda b, u: (b, u, 0, 0))
    k_spec = pl.BlockSpec((None, ups, lp, UNIT), lambda b, u: (b, u, 0, 0))
    if fox:
        extra_specs = [pl.BlockSpec((None, ups, 2, lp), lambda b, u: (b, u, 0, 0))]
    else:
        extra_specs = [pl.BlockSpec((4, DIFF_QK_DIM), lambda b, u: (0, 0)),
                       pl.BlockSpec((ups, 1, UNIT), lambda b, u: (u, 0, 0))]
    stat = pltpu.VMEM((ups, 2, 1, tq), F32)
    acc = pltpu.VMEM((ups, 2, UNIT + ONES_ROWS, tq), F32)
    score = pltpu.VMEM((ups, 2, tq, tq), F32)
    prob = pltpu.VMEM((ups, 2, tq, tq), BF16)
    return pl.pallas_call(
        functools.partial(_flash_kernel, fox=fox, tq=tq, seq=seq, lambda_init=lambda_init),
        grid=(bsz, N_UNITS // ups),
        in_specs=[q_spec, k_spec, q_spec] + extra_specs,
        out_specs=pl.BlockSpec((None, lp, ups * UNIT), lambda b, u: (b, 0, u)),
        out_shape=jax.ShapeDtypeStruct((bsz, lp, N_UNITS * UNIT), BF16),
        scratch_shapes=[stat, acc, acc, pltpu.VMEM((ups, 2, UNIT, tq), BF16),
                        score, score, prob, prob, stat, stat, stat, stat],
        compiler_params=pltpu.CompilerParams(
            dimension_semantics=("parallel", "parallel"), vmem_limit_bytes=VMEM_LIMIT),
        name="fox_flash" if fox else "diff_flash",
    )(qt, k, vt, *extra)


def _out_kernel(*refs, final, split_meta):
    oa_ref, ob_ref, z_ref, w_ref, h_ref = refs[:5]
    meta_ref = refs[5] if split_meta else None
    g_ref = refs[-2] if final else None
    o_ref = refs[-1]
    half = oa_ref.shape[-1]
    mixed_a = (oa_ref[...].astype(F32) * z_ref[:, 0:half]).astype(BF16)
    mixed_b = (ob_ref[...].astype(F32) * z_ref[:, half:2 * half]).astype(BF16)
    h = (_hidden_tile(h_ref, meta_ref, pl.program_id(1) == pl.num_programs(1) - 1)
         + jnp.dot(mixed_a, w_ref[0:half, :], preferred_element_type=F32)
         + jnp.dot(mixed_b, w_ref[half:2 * half, :], preferred_element_type=F32))
    if final:
        h = h * lax.rsqrt(jnp.mean(h * h, axis=-1, keepdims=True) + NORM_EPS) * g_ref[...]
    o_ref[...] = h


def _out_project(o_a, o_b, zg, w_out, h, meta_tile, final_g, *, tl, rows):
    bsz, _, dm = h.shape
    half = o_a.shape[-1]
    final = final_g is not None
    split_meta = meta_tile is not None
    nt = rows // tl
    row_spec = lambda width: pl.BlockSpec((None, tl, width), lambda b, i: (b, i, 0))
    const = lambda shape: pl.BlockSpec(shape, lambda b, i: (0,) * len(shape))
    in_specs = [row_spec(half), row_spec(half), row_spec(2 * half), const((2 * half, dm))]
    args = [o_a, o_b, zg, w_out, h]
    if split_meta:
        in_specs += [pl.BlockSpec((None, tl, dm), lambda b, i: (b, jnp.minimum(i, nt - 2), 0)),
                     const((tl, dm))]
        args.append(meta_tile)
    else:
        in_specs.append(row_spec(dm))
    if final:
        in_specs.append(const((1, dm)))
        args.append(final_g)
    return pl.pallas_call(
        functools.partial(_out_kernel, final=final, split_meta=split_meta),
        grid=(bsz, nt),
        in_specs=in_specs,
        out_specs=row_spec(dm),
        out_shape=jax.ShapeDtypeStruct((bsz, rows, dm), F32),
        compiler_params=pltpu.CompilerParams(
            dimension_semantics=("parallel", "parallel"), vmem_limit_bytes=VMEM_LIMIT),
        name="out_final" if final else "out_proj",
    )(*args)


def _rope_tables(seq, lp):
    half = DIFF_QK_DIM // 2
    pos = np.zeros((lp,), np.float64)
    pos[:seq] = N_META + np.arange(seq)
    pos[seq:seq + N_META] = np.arange(N_META)
    inv = ROPE_THETA ** (-np.arange(half, dtype=np.float64) / half)
    ang = pos[:, None] * np.tile(inv, UNIT // half)[None, :]
    sign = np.where(np.arange(UNIT) < UNIT // 2, -1.0, 1.0)[None, :]
    cos, sin = np.cos(ang), np.sin(ang) * sign
    q_scale = DIFF_QK_DIM ** -0.5 * LOG2E
    return tuple(jnp.asarray(t, F32) for t in (cos * q_scale, sin * q_scale, cos, sin))


def _unit_lane_order(w_qk):
    dm, width = w_qk.shape
    half = DIFF_QK_DIM // 2
    w5 = w_qk.reshape(dm, width // (4 * half), 2, 2, half)
    return w5.transpose(0, 1, 3, 2, 4).reshape(dm, width)


def _projection_weight(w_layer):
    width = N_UNITS * UNIT
    w_layer = w_layer.astype(BF16)
    sec = lambda k: w_layer[:, k * width:(k + 1) * width]
    logits = jnp.pad(w_layer[:, 8 * width:], ((0, 0), (0, UNIT - FOX_HEADS)))
    return jnp.concatenate(
        [_unit_lane_order(sec(0)), _unit_lane_order(sec(1)), sec(2), sec(4), sec(5), sec(6),
         sec(3), sec(7), logits], axis=1)


def kernel(x, meta_tokens, norm_g, w_in, b_forget, lam_q1, lam_k1, lam_q2, lam_k2, subln_g,
           w_out, final_g):
    bsz, seq, dm = x.shape
    depth = w_in.shape[0]
    tl = SEQ_TILE
    lp = seq + tl
    width = N_UNITS * UNIT
    assert seq % tl == 0 and N_META <= UNIT and w_in.shape[2] == 8 * width + FOX_HEADS

    h, meta_tile = x, jnp.pad(meta_tokens.astype(x.dtype), ((0, tl - N_META), (0, 0)))

    tabs = _rope_tables(seq, lp)
    tri = jnp.asarray(np.tril(np.ones((tl, tl), np.float32)), BF16)

    out = None
    for layer in range(depth):
        lambda_init = 0.8 - 0.6 * math.exp(-0.3 * layer)
        w = _projection_weight(w_in[layer])
        bf = jnp.pad(b_forget[layer].astype(F32), (0, UNIT - FOX_HEADS)).reshape(1, UNIT)
        (qdt, kd, vdt, qft, kf, vft, zg, cum) = _project(
            h, meta_tile, norm_g[layer].reshape(1, dm), w, bf, tabs, tri, tl=tl, lp=lp)

        lam = jnp.stack([lam_q1[layer], lam_k1[layer], lam_q2[layer], lam_k2[layer]]).astype(F32)
        o_a = _flash(qdt, kd, vdt, (lam, subln_g[layer].reshape(N_UNITS, 1, UNIT)),
                     fox=False, tq=tl, seq=seq, lambda_init=lambda_init)
        o_b = _flash(qft, kf, vft, (cum.reshape(bsz, N_UNITS, 2, lp),),
                     fox=True, tq=tl, seq=seq, lambda_init=lambda_init)

        w_o = w_out[layer].astype(BF16)
        if layer + 1 < depth:
            h = _out_project(o_a, o_b, zg, w_o, h, meta_tile, None, tl=tl, rows=lp)
            meta_tile = None
        else:
            out = _out_project(o_a, o_b, zg, w_o, h, None, final_g.reshape(1, dm),
                               tl=tl, rows=seq)
    return out
```

```python
import functools
import math

import numpy as np
import jax
import jax.numpy as jnp
from jax import lax
from jax.experimental import pallas as pl
from jax.experimental.pallas import tpu as pltpu

N_META = 16
DIFF_QK_DIM = 64
FOX_HEAD_DIM = 64
FOX_HEADS = 8
UNIT = 128
N_UNITS = 4
UNITS_PER_STEP = 2
ROPE_THETA = 10000.0
NORM_EPS = 1e-6
LOG2E = 1.0 / math.log(2.0)
NEG = -1e30
ONES_ROWS = 16

SEQ_TILE = 512
VMEM_LIMIT = 56 * 1024 * 1024

F32 = jnp.float32
BF16 = jnp.bfloat16


def _hidden_tile(h_ref, meta_ref, is_meta_tile):
    if meta_ref is None:
        return h_ref[...]
    return jnp.where(is_meta_tile, meta_ref[...], h_ref[...])


def _proj_kernel(*refs, tl, fox_scale, split_meta):
    h_ref, meta_ref = (refs[0], refs[1]) if split_meta else (refs[0], None)
    (g_ref, w_ref, bf_ref, cosq_ref, sinq_ref, cosk_ref, sink_ref, tri_ref,
     qdt_ref, kd_ref, vdt_ref, qft_ref, kf_ref, vft_ref, zg_ref, cum_ref,
     carry_ref) = refs[2 if split_meta else 1:]
    step = pl.program_id(1)

    h = _hidden_tile(h_ref, meta_ref, step == 0)
    u = h * lax.rsqrt(jnp.mean(h * h, axis=-1, keepdims=True) + NORM_EPS) * g_ref[...]
    ub = u.astype(BF16)

    def proj(lo, width):
        return jnp.dot(ub, w_ref[:, lo:lo + width], preferred_element_type=F32)

    def rope(t, cos_ref, sin_ref):
        return t * cos_ref[...] + pltpu.roll(t, 64, 1) * sin_ref[...]

    width = N_UNITS * UNIT
    dq = proj(0 * width, width)
    for un in range(N_UNITS):
        qdt_ref[un] = rope(dq[:, un * UNIT:(un + 1) * UNIT], cosq_ref, sinq_ref).T.astype(BF16)
    dk = proj(1 * width, width)
    for un in range(N_UNITS):
        kd_ref[un] = rope(dk[:, un * UNIT:(un + 1) * UNIT], cosk_ref, sink_ref).astype(BF16)
    dv = proj(2 * width, width)
    for un in range(N_UNITS):
        vdt_ref[un] = dv[:, un * UNIT:(un + 1) * UNIT].T.astype(BF16)
    fq = proj(3 * width, width) * fox_scale
    for un in range(N_UNITS):
        qft_ref[un] = fq[:, un * UNIT:(un + 1) * UNIT].T.astype(BF16)
    fk = proj(4 * width, width)
    for un in range(N_UNITS):
        kf_ref[un] = fk[:, un * UNIT:(un + 1) * UNIT].astype(BF16)
    fv = proj(5 * width, width)
    for un in range(N_UNITS):
        vft_ref[un] = fv[:, un * UNIT:(un + 1) * UNIT].T.astype(BF16)

    z = proj(6 * width, 2 * width)
    zg_ref[...] = z * (1.0 / (1.0 + jnp.exp(-z)))

    fl = proj(8 * width, UNIT) + bf_ref[...]
    ls = (jnp.minimum(fl, 0.0) - jnp.log1p(jnp.exp(-jnp.abs(fl)))) * LOG2E
    row = lax.broadcasted_iota(jnp.int32, (tl, UNIT), 0)
    ls = jnp.where(row >= jnp.where(step == 0, N_META, tl), 0.0, ls)

    @pl.when(step == 0)
    def _():
        carry_ref[...] = jnp.zeros_like(carry_ref)

    hi = ls.astype(BF16)
    r1 = ls - hi.astype(F32)
    mid = r1.astype(BF16)
    lo = (r1 - mid.astype(F32)).astype(BF16)
    tri = tri_ref[...]
    cum = (jnp.dot(tri, hi, preferred_element_type=F32)
           + jnp.dot(tri, mid, preferred_element_type=F32)
           + jnp.dot(tri, lo, preferred_element_type=F32)) + carry_ref[...]
    carry_ref[...] = cum[tl - 1:tl, :]
    cum_ref[...] = cum.T[0:FOX_HEADS, :]


def _project(h, meta_tile, norm_g, w, b_forget, tabs, tri, *, tl, lp):
    bsz, _, dm = h.shape
    nl = lp // tl
    pw = w.shape[1]
    split_meta = meta_tile is not None

    def tok(b, i):
        return (i + nl - 1) % nl

    row_spec = lambda width: pl.BlockSpec((None, tl, width), lambda b, i: (b, tok(b, i), 0))
    unit_rows = pl.BlockSpec((None, N_UNITS, tl, UNIT), lambda b, i: (b, 0, tok(b, i), 0))
    unit_cols = pl.BlockSpec((None, N_UNITS, UNIT, tl), lambda b, i: (b, 0, 0, tok(b, i)))
    tab_spec = pl.BlockSpec((tl, UNIT), lambda b, i: (tok(b, i), 0))
    const = lambda shape: pl.BlockSpec(shape, lambda b, i: (0,) * len(shape))

    unit_t = jax.ShapeDtypeStruct((bsz, N_UNITS, UNIT, lp), BF16)
    unit_n = jax.ShapeDtypeStruct((bsz, N_UNITS, lp, UNIT), BF16)
    if split_meta:
        h_specs = [pl.BlockSpec((None, tl, dm), lambda b, i: (b, jnp.minimum(tok(b, i), nl - 2), 0)),
                   const((tl, dm))]
        h_args = [h, meta_tile]
    else:
        h_specs, h_args = [row_spec(dm)], [h]
    return pl.pallas_call(
        functools.partial(_proj_kernel, tl=tl, fox_scale=FOX_HEAD_DIM ** -0.5 * LOG2E,
                          split_meta=split_meta),
        grid=(bsz, nl),
        in_specs=h_specs + [const((1, dm)), const((dm, pw)), const((1, UNIT)),
                            tab_spec, tab_spec, tab_spec, tab_spec, const((tl, tl))],
        out_specs=[unit_cols, unit_rows, unit_cols, unit_cols, unit_rows, unit_cols,
                   row_spec(2 * N_UNITS * UNIT),
                   pl.BlockSpec((None, FOX_HEADS, tl), lambda b, i: (b, 0, tok(b, i)))],
        out_shape=[unit_t, unit_n, unit_t, unit_t, unit_n, unit_t,
                   jax.ShapeDtypeStruct((bsz, lp, 2 * N_UNITS * UNIT), F32),
                   jax.ShapeDtypeStruct((bsz, FOX_HEADS, lp), F32)],
        scratch_shapes=[pltpu.VMEM((1, UNIT), F32)],
        compiler_params=pltpu.CompilerParams(
            dimension_semantics=("arbitrary", "arbitrary"), vmem_limit_bytes=VMEM_LIMIT),
        name="proj",
    )(*h_args, norm_g, w, b_forget, *tabs, tri)


def _flash_kernel(*refs, fox, tq, seq, lambda_init):
    if fox:
        qt_ref, k_ref, vt_ref, cum_ref, o_ref = refs[:5]
        n_in = 5
    else:
        qt_ref, k_ref, vt_ref, lam_ref, g_ref, o_ref = refs[:6]
        n_in = 6
    m_sc, acc_sc, acc0_sc, q_sc = refs[n_in:n_in + 4]
    s_buf, p_buf, alpha_buf, mblk_buf = (
        refs[n_in + 4 + 2 * n:n_in + 6 + 2 * n] for n in range(4))
    units = range(UNITS_PER_STEP)
    tk = tq
    nq = seq // tq
    n_steps = nq * (nq + 1) // 2

    def tile_start(t):
        return pl.multiple_of(t * tq, tq)

    def load_queries(un, i):
        feat = lax.broadcasted_iota(jnp.int32, (UNIT, tq), 0)
        second = (feat >= FOX_HEAD_DIM) if fox else ((feat // 32) % 2 == 1)
        qt = qt_ref[un, :, pl.ds(tile_start(i), tq)].astype(F32)
        q_sc[un, 0] = jnp.where(second, 0.0, qt).astype(BF16)
        q_sc[un, 1] = jnp.where(second, qt, 0.0).astype(BF16)

    all_lanes = slice(0, tq)
    lane_halves = (slice(0, tq // 2), slice(tq // 2, tq))
    plain_pieces = (all_lanes,) if fox else lane_halves

    def key_bias(un, k_off, k_len, c):
        if not fox:
            return None
        return jnp.broadcast_to(cum_ref[un, c:c + 1, pl.ds(k_off, k_len)], (UNIT, k_len)).T

    def scores(un, k_off, k_len, c, lanes=all_lanes, ck=None):
        s = jnp.dot(k_ref[un, pl.ds(k_off, k_len), :], q_sc[un, c, :, lanes],
                    preferred_element_type=F32)
        if fox:
            if ck is None:
                ck = key_bias(un, k_off, k_len, c)
            s = jnp.concatenate(
                [s[:, b * UNIT:(b + 1) * UNIT] - ck for b in range(s.shape[1] // UNIT)], axis=1)
        return s

    def issue_scores(un, j, par, cs=(0, 1), lanes=all_lanes, ck=None):
        for c in cs:
            s = scores(un, tile_start(j), tk, c, lanes, ck)
            s_buf[par][un, c, :, lanes] = s
            mblk_buf[par][un, c, :, lanes] = jnp.max(s, axis=0, keepdims=True)

    def softmax_core(un, i, c, s, m_blk, m_old, lanes=all_lanes):
        if fox:
            cq = cum_ref[un, c:c + 1, pl.ds(tile_start(i) + lanes.start, lanes.stop - lanes.start)]
            m_new = jnp.maximum(m_old, m_blk + cq)
            shift = m_new - cq
        else:
            m_new = jnp.maximum(m_old, m_blk)
            shift = m_new
        return m_new, jnp.exp2(m_old - m_new), jnp.exp2(s - shift)

    def softmax(un, i, par, cs=(0, 1), lanes=all_lanes):
        for c in cs:
            m_new, alpha, p = softmax_core(un, i, c, s_buf[par][un, c, :, lanes],
                                           mblk_buf[par][un, c, :, lanes],
                                           m_sc[un, c, :, lanes], lanes)
            alpha_buf[par][un, c, :, lanes] = alpha
            p_buf[par][un, c, :, lanes] = p.astype(BF16)
            m_sc[un, c, :, lanes] = m_new

    def softmax_diagonal(un, i, par, c):
        half = tq // 2
        below = (lax.broadcasted_iota(jnp.int32, (half, half), 0)
                 <= lax.broadcasted_iota(jnp.int32, (half, half), 1))
        for part, lanes in enumerate(lane_halves):
            s = jnp.where(below, s_buf[par][un, c, lanes, lanes], NEG)
            if part == 1:
                s = jnp.concatenate([s_buf[par][un, c, 0:half, lanes], s], axis=0)
            m_new, alpha, p = softmax_core(un, i, c, s, jnp.max(s, axis=0, keepdims=True),
                                           m_sc[un, c, :, lanes], lanes)
            alpha_buf[par][un, c, :, lanes] = alpha
            m_sc[un, c, :, lanes] = m_new
            if part == 0:
                p_buf[par][un, c, 0:half, lanes] = p.astype(BF16)
                p_buf[par][un, c, half:tk, lanes] = jnp.zeros((tk - half, half), BF16)
            else:
                p_buf[par][un, c, :, lanes] = p.astype(BF16)

    def pv_update(un, j, par, cs=(0, 1), lanes=all_lanes):
        vtt = jnp.concatenate(
            [vt_ref[un, :, pl.ds(tile_start(j), tk)], jnp.ones((ONES_ROWS, tk), BF16)], axis=0)
        for c in cs:
            acc_sc[un, c, :, lanes] = (
                alpha_buf[par][un, c, :, lanes] * acc_sc[un, c, :, lanes]
                + jnp.dot(vtt, p_buf[par][un, c, :, lanes], preferred_element_type=F32))

    def meta_scores(un, causal):
        key = lax.broadcasted_iota(jnp.int32, (UNIT, tq), 0)
        mask = key < N_META
        if causal:
            mask = jnp.logical_and(mask, key <= lax.broadcasted_iota(jnp.int32, (UNIT, tq), 1))
        return [jnp.where(mask, scores(un, seq, UNIT, c), NEG) for c in range(2)]

    def meta_softmax(un, i, s_pair):
        p_pair = []
        for c, s in enumerate(s_pair):
            m_new, _, p = softmax_core(un, i, c, s, jnp.max(s, axis=0, keepdims=True),
                                       jnp.full((1, tq), NEG, F32))
            m_sc[un, c] = m_new
            p_pair.append(p.astype(BF16))
        return p_pair

    def meta_values(un, p_pair):
        vtt = vt_ref[un, :, pl.ds(seq, UNIT)]
        for c, p in enumerate(p_pair):
            acc0_sc[un, c, 0:UNIT] = jnp.dot(vtt, p, preferred_element_type=F32)
            denom = jnp.sum(p.astype(F32), axis=0, keepdims=True)
            acc0_sc[un, c, UNIT:UNIT + ONES_ROWS] = jnp.broadcast_to(denom, (ONES_ROWS, tq))

    def start_row(un, i, causal):
        meta_values(un, meta_softmax(un, i, meta_scores(un, causal)))

    def normalized(un, c):
        acc = acc_sc[un, c]
        return acc[:UNIT] * (1.0 / acc[UNIT:UNIT + 1])

    def finalize(un, i):
        o1 = normalized(un, 0)
        o2 = normalized(un, 1)
        if fox:
            o = jnp.concatenate([o1[:FOX_HEAD_DIM], o2[FOX_HEAD_DIM:]], axis=0).T
        else:
            lam = (jnp.exp(jnp.sum(lam_ref[0:1, :] * lam_ref[1:2, :], axis=-1, keepdims=True))
                   - jnp.exp(jnp.sum(lam_ref[2:3, :] * lam_ref[3:4, :], axis=-1, keepdims=True))
                   + lambda_init)
            o = (o1 - lam * o2).T
            o = o * lax.rsqrt(jnp.mean(o * o, axis=-1, keepdims=True) + NORM_EPS) * g_ref[un]
            o = o * (1.0 - lambda_init)
        o_ref[pl.ds(tile_start(i), tq), un * UNIT:(un + 1) * UNIT] = o.astype(BF16)

    def diag_softmax(i, par):
        for un in units:
            for c in range(2):
                softmax_diagonal(un, i, par, c)
                pv_update(un, jnp.maximum(i - 1, 0), 1 - par, cs=(c,))

    def next_row_scores(i, par):
        for un in units:
            load_queries(un, i + 1)
        s_meta = [meta_scores(un, causal=False) for un in units]
        for un in units:
            issue_scores(un, 0, 1 - par)
        return s_meta

    def next_row_start(i, s_meta):
        p_meta = [meta_softmax(un, i + 1, s_meta[un]) for un in units]
        for un in units:
            meta_values(un, p_meta[un])

    def step(i, j, par):
        @pl.when(jnp.logical_and(j > 0, j < i))
        def _():
            for un in units:
                for c in range(2):
                    ck = key_bias(un, tile_start(j + 1), tk, c)
                    for lanes in plain_pieces:
                        softmax(un, i, par, cs=(c,), lanes=lanes)
                        pv_update(un, j - 1, 1 - par, cs=(c,), lanes=lanes)
                        issue_scores(un, j + 1, 1 - par, cs=(c,), lanes=lanes, ck=ck)

        @pl.when(jnp.logical_and(j == 0, i > 0))
        def _():
            for un in units:
                for c in range(2):
                    softmax(un, i, par, cs=(c,))
                    pv_update(un, i - 1, 1 - par, cs=(c,))
                    issue_scores(un, 1, 1 - par, cs=(c,))
            for un in units:
                finalize(un, i - 1)
                acc_sc[un] = acc0_sc[un]

        @pl.when(j == i)
        def _():
            s_meta = next_row_scores(i, par)
            diag_softmax(i, par)
            next_row_start(i, s_meta)

        wrap = j == i
        return jnp.where(wrap, i + 1, i), jnp.where(wrap, 0, j + 1)

    for un in units:
        load_queries(un, 0)
        start_row(un, 0, causal=False)
        acc_sc[un] = acc0_sc[un]
        issue_scores(un, 0, 0)
    alpha_buf[1][...] = jnp.ones_like(alpha_buf[1])
    p_buf[1][...] = jnp.zeros_like(p_buf[1])

    def body(_, ij):
        i, j = step(ij[0], ij[1], 0)
        return step(i, j, 1)

    i, j = lax.fori_loop(0, (n_steps - 2) // 2, body, (jnp.int32(0), jnp.int32(0)))
    step(i, j, 0)
    diag_softmax(nq - 1, 1)
    for un in units:
        pv_update(un, nq - 1, 1)
        finalize(un, nq - 1)

    for un in units:
        load_queries(un, nq)
        start_row(un, nq, causal=True)
        acc_sc[un] = acc0_sc[un]
        finalize(un, nq)


def _flash(qt, k, vt, extra, *, fox, tq, seq, lambda_init):
    bsz, _, lp, _ = k.shape
    nq = seq // tq
    ups = UNITS_PER_STEP
    assert lp == seq + tq and (nq * (nq + 1) // 2) % 2 == 0 and N_UNITS % ups == 0
    q_spec = pl.BlockSpec((None, ups, UNIT, lp), lambda b, u: (b, u, 0, 0))
    k_spec = pl.BlockSpec((None, ups, lp, UNIT), lambda b, u: (b, u, 0, 0))
    if fox:
        extra_specs = [pl.BlockSpec((None, ups, 2, lp), lambda b, u: (b, u, 0, 0))]
    else:
        extra_specs = [pl.BlockSpec((4, DIFF_QK_DIM), lambda b, u: (0, 0)),
                       pl.BlockSpec((ups, 1, UNIT), lambda b, u: (u, 0, 0))]
    stat = pltpu.VMEM((ups, 2, 1, tq), F32)
    acc = pltpu.VMEM((ups, 2, UNIT + ONES_ROWS, tq), F32)
    score = pltpu.VMEM((ups, 2, tq, tq), F32)
    prob = pltpu.VMEM((ups, 2, tq, tq), BF16)
    return pl.pallas_call(
        functools.partial(_flash_kernel, fox=fox, tq=tq, seq=seq, lambda_init=lambda_init),
        grid=(bsz, N_UNITS // ups),
        in_specs=[q_spec, k_spec, q_spec] + extra_specs,
        out_specs=pl.BlockSpec((None, lp, ups * UNIT), lambda b, u: (b, 0, u)),
        out_shape=jax.ShapeDtypeStruct((bsz, lp, N_UNITS * UNIT), BF16),
        scratch_shapes=[stat, acc, acc, pltpu.VMEM((ups, 2, UNIT, tq), BF16),
                        score, score, prob, prob, stat, stat, stat, stat],
        compiler_params=pltpu.CompilerParams(
            dimension_semantics=("parallel", "parallel"), vmem_limit_bytes=VMEM_LIMIT),
        name="fox_flash" if fox else "diff_flash",
    )(qt, k, vt, *extra)


def _out_kernel(*refs, final, split_meta):
    oa_ref, ob_ref, z_ref, w_ref, h_ref = refs[:5]
    meta_ref = refs[5] if split_meta else None
    g_ref = refs[-2] if final else None
    o_ref = refs[-1]
    half = oa_ref.shape[-1]
    mixed_a = (oa_ref[...].astype(F32) * z_ref[:, 0:half]).astype(BF16)
    mixed_b = (ob_ref[...].astype(F32) * z_ref[:, half:2 * half]).astype(BF16)
    h = (_hidden_tile(h_ref, meta_ref, pl.program_id(1) == pl.num_programs(1) - 1)
         + jnp.dot(mixed_a, w_ref[0:half, :], preferred_element_type=F32)
         + jnp.dot(mixed_b, w_ref[half:2 * half, :], preferred_element_type=F32))
    if final:
        h = h * lax.rsqrt(jnp.mean(h * h, axis=-1, keepdims=True) + NORM_EPS) * g_ref[...]
    o_ref[...] = h


def _out_project(o_a, o_b, zg, w_out, h, meta_tile, final_g, *, tl, rows):
    bsz, _, dm = h.shape
    half = o_a.shape[-1]
    final = final_g is not None
    split_meta = meta_tile is not None
    nt = rows // tl
    row_spec = lambda width: pl.BlockSpec((None, tl, width), lambda b, i: (b, i, 0))
    const = lambda shape: pl.BlockSpec(shape, lambda b, i: (0,) * len(shape))
    in_specs = [row_spec(half), row_spec(half), row_spec(2 * half), const((2 * half, dm))]
    args = [o_a, o_b, zg, w_out, h]
    if split_meta:
        in_specs += [pl.BlockSpec((None, tl, dm), lambda b, i: (b, jnp.minimum(i, nt - 2), 0)),
                     const((tl, dm))]
        args.append(meta_tile)
    else:
        in_specs.append(row_spec(dm))
    if final:
        in_specs.append(const((1, dm)))
        args.append(final_g)
    return pl.pallas_call(
        functools.partial(_out_kernel, final=final, split_meta=split_meta),
        grid=(bsz, nt),
        in_specs=in_specs,
        out_specs=row_spec(dm),
        out_shape=jax.ShapeDtypeStruct((bsz, rows, dm), F32),
        compiler_params=pltpu.CompilerParams(
            dimension_semantics=("parallel", "parallel"), vmem_limit_bytes=VMEM_LIMIT),
        name="out_final" if final else "out_proj",
    )(*args)


def _rope_tables(seq, lp):
    half = DIFF_QK_DIM // 2
    pos = np.zeros((lp,), np.float64)
    pos[:seq] = N_META + np.arange(seq)
    pos[seq:seq + N_META] = np.arange(N_META)
    inv = ROPE_THETA ** (-np.arange(half, dtype=np.float64) / half)
    ang = pos[:, None] * np.tile(inv, UNIT // half)[None, :]
    sign = np.where(np.arange(UNIT) < UNIT // 2, -1.0, 1.0)[None, :]
    cos, sin = np.cos(ang), np.sin(ang) * sign
    q_scale = DIFF_QK_DIM ** -0.5 * LOG2E
    return tuple(jnp.asarray(t, F32) for t in (cos * q_scale, sin * q_scale, cos, sin))


def _unit_lane_order(w_qk):
    dm, width = w_qk.shape
    half = DIFF_QK_DIM // 2
    w5 = w_qk.reshape(dm, width // (4 * half), 2, 2, half)
    return w5.transpose(0, 1, 3, 2, 4).reshape(dm, width)


def _projection_weight(w_layer):
    width = N_UNITS * UNIT
    w_layer = w_layer.astype(BF16)
    sec = lambda k: w_layer[:, k * width:(k + 1) * width]
    logits = jnp.pad(w_layer[:, 8 * width:], ((0, 0), (0, UNIT - FOX_HEADS)))
    return jnp.concatenate(
        [_unit_lane_order(sec(0)), _unit_lane_order(sec(1)), sec(2), sec(4), sec(5), sec(6),
         sec(3), sec(7), logits], axis=1)


def kernel(x, meta_tokens, norm_g, w_in, b_forget, lam_q1, lam_k1, lam_q2, lam_k2, subln_g,
           w_out, final_g):
    bsz, seq, dm = x.shape
    depth = w_in.shape[0]
    tl = SEQ_TILE
    lp = seq + tl
    width = N_UNITS * UNIT
    assert seq % tl == 0 and N_META <= UNIT and w_in.shape[2] == 8 * width + FOX_HEADS

    h, meta_tile = x, jnp.pad(meta_tokens.astype(x.dtype), ((0, tl - N_META), (0, 0)))

    tabs = _rope_tables(seq, lp)
    tri = jnp.asarray(np.tril(np.ones((tl, tl), np.float32)), BF16)

    out = None
    for layer in range(depth):
        lambda_init = 0.8 - 0.6 * math.exp(-0.3 * layer)
        w = _projection_weight(w_in[layer])
        bf = jnp.pad(b_forget[layer].astype(F32), (0, UNIT - FOX_HEADS)).reshape(1, UNIT)
        (qdt, kd, vdt, qft, kf, vft, zg, cum) = _project(
            h, meta_tile, norm_g[layer].reshape(1, dm), w, bf, tabs, tri, tl=tl, lp=lp)

        lam = jnp.stack([lam_q1[layer], lam_k1[layer], lam_q2[layer], lam_k2[layer]]).astype(F32)
        o_a = _flash(qdt, kd, vdt, (lam, subln_g[layer].reshape(N_UNITS, 1, UNIT)),
                     fox=False, tq=tl, seq=seq, lambda_init=lambda_init)
        o_b = _flash(qft, kf, vft, (cum.reshape(bsz, N_UNITS, 2, lp),),
                     fox=True, tq=tl, seq=seq, lambda_init=lambda_init)

        w_o = w_out[layer].astype(BF16)
        if layer + 1 < depth:
            h = _out_project(o_a, o_b, zg, w_o, h, meta_tile, None, tl=tl, rows=lp)
            meta_tile = None
        else:
            out = _out_project(o_a, o_b, zg, w_o, h, None, final_g.reshape(1, dm),
                               tl=tl, rows=seq)
    return out
```

```python
import functools
import math

import numpy as np
import jax
import jax.numpy as jnp
from jax import lax
from jax.experimental import pallas as pl
from jax.experimental.pallas import tpu as pltpu

N_META = 16
DIFF_QK_DIM = 64
FOX_HEAD_DIM = 64
FOX_HEADS = 8
UNIT = 128
N_UNITS = 4
UNITS_PER_STEP = 2
ROPE_THETA = 10000.0
NORM_EPS = 1e-6
LOG2E = 1.0 / math.log(2.0)
NEG = -1e30
ONES_ROWS = 16

SEQ_TILE = 512
VMEM_LIMIT = 56 * 1024 * 1024

F32 = jnp.float32
BF16 = jnp.bfloat16


def _hidden_tile(h_ref, meta_ref, is_meta_tile):
    if meta_ref is None:
        return h_ref[...]
    return jnp.where(is_meta_tile, meta_ref[...], h_ref[...])


def _proj_kernel(*refs, tl, fox_scale, split_meta):
    h_ref, meta_ref = (refs[0], refs[1]) if split_meta else (refs[0], None)
    (g_ref, w_ref, bf_ref, cosq_ref, sinq_ref, cosk_ref, sink_ref, tri_ref,
     qdt_ref, kd_ref, vdt_ref, qft_ref, kf_ref, vft_ref, zg_ref, cum_ref,
     carry_ref) = refs[2 if split_meta else 1:]
    step = pl.program_id(1)

    @pl.when(step == 0)
    def _():
        carry_ref[...] = jnp.zeros_like(carry_ref)

    h = _hidden_tile(h_ref, meta_ref, step == 0)
    u = h * lax.rsqrt(jnp.mean(h * h, axis=-1, keepdims=True) + NORM_EPS) * g_ref[...]
    ub = u.astype(BF16)

    def proj(lo, width):
        return jnp.dot(ub, w_ref[:, lo:lo + width], preferred_element_type=F32)

    def rope(t, cos_ref, sin_ref):
        return t * cos_ref[...] + pltpu.roll(t, 64, 1) * sin_ref[...]

    width = N_UNITS * UNIT

    fl = proj(8 * width, UNIT) + bf_ref[...]
    ls = (jnp.minimum(fl, 0.0) - jnp.log1p(jnp.exp(-jnp.abs(fl)))) * LOG2E
    row = lax.broadcasted_iota(jnp.int32, (tl, UNIT), 0)
    ls = jnp.where(row >= jnp.where(step == 0, N_META, tl), 0.0, ls)

    hi = ls.astype(BF16)
    r1 = ls - hi.astype(F32)
    mid = r1.astype(BF16)
    lo = (r1 - mid.astype(F32)).astype(BF16)
    tri = tri_ref[...]
    cum = (jnp.dot(tri, hi, preferred_element_type=F32)
           + jnp.dot(tri, mid, preferred_element_type=F32)
           + jnp.dot(tri, lo, preferred_element_type=F32)) + carry_ref[...]
    carry_ref[...] = cum[tl - 1:tl, :]
    cum_ref[...] = cum.T[0:FOX_HEADS, :]

    dq = proj(0 * width, width)
    for un in range(N_UNITS):
        qdt_ref[un] = rope(dq[:, un * UNIT:(un + 1) * UNIT], cosq_ref, sinq_ref).T.astype(BF16)
    dk = proj(1 * width, width)
    for un in range(N_UNITS):
        kd_ref[un] = rope(dk[:, un * UNIT:(un + 1) * UNIT], cosk_ref, sink_ref).astype(BF16)
    dv = proj(2 * width, width)
    for un in range(N_UNITS):
        vdt_ref[un] = dv[:, un * UNIT:(un + 1) * UNIT].T.astype(BF16)
    fq = proj(3 * width, width) * fox_scale
    for un in range(N_UNITS):
        qft_ref[un] = fq[:, un * UNIT:(un + 1) * UNIT].T.astype(BF16)
    fk = proj(4 * width, width)
    for un in range(N_UNITS):
        kf_ref[un] = fk[:, un * UNIT:(un + 1) * UNIT].astype(BF16)
    fv = proj(5 * width, width)
    for un in range(N_UNITS):
        vft_ref[un] = fv[:, un * UNIT:(un + 1) * UNIT].T.astype(BF16)

    z = proj(6 * width, 2 * width)
    zg_ref[...] = z * (1.0 / (1.0 + jnp.exp(-z)))


def _project(h, meta_tile, norm_g, w, b_forget, tabs, tri, *, tl, lp):
    bsz, _, dm = h.shape
    nl = lp // tl
    pw = w.shape[1]
    split_meta = meta_tile is not None

    def tok(b, i):
        return (i + nl - 1) % nl

    row_spec = lambda width: pl.BlockSpec((None, tl, width), lambda b, i: (b, tok(b, i), 0))
    unit_rows = pl.BlockSpec((None, N_UNITS, tl, UNIT), lambda b, i: (b, 0, tok(b, i), 0))
    unit_cols = pl.BlockSpec((None, N_UNITS, UNIT, tl), lambda b, i: (b, 0, 0, tok(b, i)))
    tab_spec = pl.BlockSpec((tl, UNIT), lambda b, i: (tok(b, i), 0))
    const = lambda shape: pl.BlockSpec(shape, lambda b, i: (0,) * len(shape))

    unit_t = jax.ShapeDtypeStruct((bsz, N_UNITS, UNIT, lp), BF16)
    unit_n = jax.ShapeDtypeStruct((bsz, N_UNITS, lp, UNIT), BF16)
    if split_meta:
        h_specs = [pl.BlockSpec((None, tl, dm), lambda b, i: (b, jnp.minimum(tok(b, i), nl - 2), 0)),
                   const((tl, dm))]
        h_args = [h, meta_tile]
    else:
        h_specs, h_args = [row_spec(dm)], [h]
    return pl.pallas_call(
        functools.partial(_proj_kernel, tl=tl, fox_scale=FOX_HEAD_DIM ** -0.5 * LOG2E,
                          split_meta=split_meta),
        grid=(bsz, nl),
        in_specs=h_specs + [const((1, dm)), const((dm, pw)), const((1, UNIT)),
                            tab_spec, tab_spec, tab_spec, tab_spec, const((tl, tl))],
        out_specs=[unit_cols, unit_rows, unit_cols, unit_cols, unit_rows, unit_cols,
                   row_spec(2 * N_UNITS * UNIT),
                   pl.BlockSpec((None, FOX_HEADS, tl), lambda b, i: (b, 0, tok(b, i)))],
        out_shape=[unit_t, unit_n, unit_t, unit_t, unit_n, unit_t,
                   jax.ShapeDtypeStruct((bsz, lp, 2 * N_UNITS * UNIT), F32),
                   jax.ShapeDtypeStruct((bsz, FOX_HEADS, lp), F32)],
        scratch_shapes=[pltpu.VMEM((1, UNIT), F32)],
        compiler_params=pltpu.CompilerParams(
            dimension_semantics=("arbitrary", "arbitrary"), vmem_limit_bytes=VMEM_LIMIT),
        name="proj",
    )(*h_args, norm_g, w, b_forget, *tabs, tri)


def _flash_kernel(*refs, fox, tq, seq, lambda_init):
    if fox:
        qt_ref, k_ref, vt_ref, cum_ref, o_ref = refs[:5]
        n_in = 5
    else:
        qt_ref, k_ref, vt_ref, lam_ref, g_ref, o_ref = refs[:6]
        n_in = 6
    m_sc, acc_sc, acc0_sc, q_sc = refs[n_in:n_in + 4]
    s_buf, p_buf, alpha_buf, mblk_buf = (
        refs[n_in + 4 + 2 * n:n_in + 6 + 2 * n] for n in range(4))
    units = range(UNITS_PER_STEP)
    tk = tq
    nq = seq // tq
    n_steps = nq * (nq + 1) // 2

    def tile_start(t):
        return pl.multiple_of(t * tq, tq)

    def load_queries(un, i):
        feat = lax.broadcasted_iota(jnp.int32, (UNIT, tq), 0)
        second = (feat >= FOX_HEAD_DIM) if fox else ((feat // 32) % 2 == 1)
        qt = qt_ref[un, :, pl.ds(tile_start(i), tq)].astype(F32)
        q_sc[un, 0] = jnp.where(second, 0.0, qt).astype(BF16)
        q_sc[un, 1] = jnp.where(second, qt, 0.0).astype(BF16)

    all_lanes = slice(0, tq)
    lane_halves = (slice(0, tq // 2), slice(tq // 2, tq))
    plain_pieces = (all_lanes,) if fox else lane_halves

    def key_bias(un, k_off, k_len, c):
        if not fox:
            return None
        return jnp.broadcast_to(cum_ref[un, c:c + 1, pl.ds(k_off, k_len)], (UNIT, k_len)).T

    def scores(un, k_off, k_len, c, lanes=all_lanes, ck=None):
        s = jnp.dot(k_ref[un, pl.ds(k_off, k_len), :], q_sc[un, c, :, lanes],
                    preferred_element_type=F32)
        if fox:
            if ck is None:
                ck = key_bias(un, k_off, k_len, c)
            s = jnp.concatenate(
                [s[:, b * UNIT:(b + 1) * UNIT] - ck for b in range(s.shape[1] // UNIT)], axis=1)
        return s

    def issue_scores(un, j, par, cs=(0, 1), lanes=all_lanes, ck=None):
        for c in cs:
            s = scores(un, tile_start(j), tk, c, lanes, ck)
            s_buf[par][un, c, :, lanes] = s
            mblk_buf[par][un, c, :, lanes] = jnp.max(s, axis=0, keepdims=True)

    def softmax_core(un, i, c, s, m_blk, m_old, lanes=all_lanes):
        if fox:
            cq = cum_ref[un, c:c + 1, pl.ds(tile_start(i) + lanes.start, lanes.stop - lanes.start)]
            m_new = jnp.maximum(m_old, m_blk + cq)
            shift = m_new - cq
        else:
            m_new = jnp.maximum(m_old, m_blk)
            shift = m_new
        return m_new, jnp.exp2(m_old - m_new), jnp.exp2(s - shift)

    def softmax(un, i, par, cs=(0, 1), lanes=all_lanes):
        for c in cs:
            m_new, alpha, p = softmax_core(un, i, c, s_buf[par][un, c, :, lanes],
                                           mblk_buf[par][un, c, :, lanes],
                                           m_sc[un, c, :, lanes], lanes)
            alpha_buf[par][un, c, :, lanes] = alpha
            p_buf[par][un, c, :, lanes] = p.astype(BF16)
            m_sc[un, c, :, lanes] = m_new

    def softmax_diagonal(un, i, par, c):
        half = tq // 2
        below = (lax.broadcasted_iota(jnp.int32, (half, half), 0)
                 <= lax.broadcasted_iota(jnp.int32, (half, half), 1))
        for part, lanes in enumerate(lane_halves):
            s = jnp.where(below, s_buf[par][un, c, lanes, lanes], NEG)
            if part == 1:
                s = jnp.concatenate([s_buf[par][un, c, 0:half, lanes], s], axis=0)
            m_new, alpha, p = softmax_core(un, i, c, s, jnp.max(s, axis=0, keepdims=True),
                                           m_sc[un, c, :, lanes], lanes)
            alpha_buf[par][un, c, :, lanes] = alpha
            m_sc[un, c, :, lanes] = m_new
            if part == 0:
                p_buf[par][un, c, 0:half, lanes] = p.astype(BF16)
                p_buf[par][un, c, half:tk, lanes] = jnp.zeros((tk - half, half), BF16)
            else:
                p_buf[par][un, c, :, lanes] = p.astype(BF16)

    def pv_update(un, j, par, cs=(0, 1), lanes=all_lanes):
        vtt = jnp.concatenate(
            [vt_ref[un, :, pl.ds(tile_start(j), tk)], jnp.ones((ONES_ROWS, tk), BF16)], axis=0)
        for c in cs:
            acc_sc[un, c, :, lanes] = (
                alpha_buf[par][un, c, :, lanes] * acc_sc[un, c, :, lanes]
                + jnp.dot(vtt, p_buf[par][un, c, :, lanes], preferred_element_type=F32))

    def meta_scores(un, causal):
        key = lax.broadcasted_iota(jnp.int32, (UNIT, tq), 0)
        mask = key < N_META
        if causal:
            mask = jnp.logical_and(mask, key <= lax.broadcasted_iota(jnp.int32, (UNIT, tq), 1))
        return [jnp.where(mask, scores(un, seq, UNIT, c), NEG) for c in range(2)]

    def meta_softmax(un, i, s_pair):
        p_pair = []
        for c, s in enumerate(s_pair):
            m_new, _, p = softmax_core(un, i, c, s, jnp.max(s, axis=0, keepdims=True),
                                       jnp.full((1, tq), NEG, F32))
            m_sc[un, c] = m_new
            p_pair.append(p.astype(BF16))
        return p_pair

    def meta_values(un, p_pair):
        vtt = vt_ref[un, :, pl.ds(seq, UNIT)]
        for c, p in enumerate(p_pair):
            acc0_sc[un, c, 0:UNIT] = jnp.dot(vtt, p, preferred_element_type=F32)
            denom = jnp.sum(p.astype(F32), axis=0, keepdims=True)
            acc0_sc[un, c, UNIT:UNIT + ONES_ROWS] = jnp.broadcast_to(denom, (ONES_ROWS, tq))

    def start_row(un, i, causal):
        meta_values(un, meta_softmax(un, i, meta_scores(un, causal)))

    def normalized(un, c):
        acc = acc_sc[un, c]
        return acc[:UNIT] * (1.0 / acc[UNIT:UNIT + 1])

    def finalize(un, i):
        o1 = normalized(un, 0)
        o2 = normalized(un, 1)
        if fox:
            o = jnp.concatenate([o1[:FOX_HEAD_DIM], o2[FOX_HEAD_DIM:]], axis=0).T
        else:
            lam = (jnp.exp(jnp.sum(lam_ref[0:1, :] * lam_ref[1:2, :], axis=-1, keepdims=True))
                   - jnp.exp(jnp.sum(lam_ref[2:3, :] * lam_ref[3:4, :], axis=-1, keepdims=True))
                   + lambda_init)
            o = (o1 - lam * o2).T
            o = o * lax.rsqrt(jnp.mean(o * o, axis=-1, keepdims=True) + NORM_EPS) * g_ref[un]
            o = o * (1.0 - lambda_init)
        o_ref[pl.ds(tile_start(i), tq), un * UNIT:(un + 1) * UNIT] = o.astype(BF16)

    def diag_softmax(i, par):
        for un in units:
            for c in range(2):
                softmax_diagonal(un, i, par, c)
                pv_update(un, jnp.maximum(i - 1, 0), 1 - par, cs=(c,))

    def next_row_scores(i, par):
        for un in units:
            load_queries(un, i + 1)
        s_meta = [meta_scores(un, causal=False) for un in units]
        for un in units:
            issue_scores(un, 0, 1 - par)
        return s_meta

    def next_row_start(i, s_meta):
        p_meta = [meta_softmax(un, i + 1, s_meta[un]) for un in units]
        for un in units:
            meta_values(un, p_meta[un])

    def step(i, j, par):
        @pl.when(jnp.logical_and(j > 0, j < i))
        def _():
            for un in units:
                for c in range(2):
                    ck = key_bias(un, tile_start(j + 1), tk, c)
                    for lanes in plain_pieces:
                        softmax(un, i, par, cs=(c,), lanes=lanes)
                        pv_update(un, j - 1, 1 - par, cs=(c,), lanes=lanes)
                        issue_scores(un, j + 1, 1 - par, cs=(c,), lanes=lanes, ck=ck)

        @pl.when(jnp.logical_and(j == 0, i > 0))
        def _():
            for un in units:
                for c in range(2):
                    softmax(un, i, par, cs=(c,))
                    pv_update(un, i - 1, 1 - par, cs=(c,))
                    issue_scores(un, 1, 1 - par, cs=(c,))
            for un in units:
                finalize(un, i - 1)
                acc_sc[un] = acc0_sc[un]

        @pl.when(j == i)
        def _():
            s_meta = next_row_scores(i, par)
            diag_softmax(i, par)
            next_row_start(i, s_meta)

        wrap = j == i
        return jnp.where(wrap, i + 1, i), jnp.where(wrap, 0, j + 1)

    for un in units:
        load_queries(un, 0)
        start_row(un, 0, causal=False)
        acc_sc[un] = acc0_sc[un]
        issue_scores(un, 0, 0)
    alpha_buf[1][...] = jnp.ones_like(alpha_buf[1])
    p_buf[1][...] = jnp.zeros_like(p_buf[1])

    def body(_, ij):
        i, j = step(ij[0], ij[1], 0)
        return step(i, j, 1)

    i, j = lax.fori_loop(0, (n_steps - 2) // 2, body, (jnp.int32(0), jnp.int32(0)))
    step(i, j, 0)
    diag_softmax(nq - 1, 1)
    for un in units:
        pv_update(un, nq - 1, 1)
        finalize(un, nq - 1)

    for un in units:
        load_queries(un, nq)
        start_row(un, nq, causal=True)
        acc_sc[un] = acc0_sc[un]
        finalize(un, nq)


def _flash(qt, k, vt, extra, *, fox, tq, seq, lambda_init):
    bsz, _, lp, _ = k.shape
    nq = seq // tq
    ups = UNITS_PER_STEP
    assert lp == seq + tq and (nq * (nq + 1) // 2) % 2 == 0 and N_UNITS % ups == 0
    q_spec = pl.BlockSpec((None, ups, UNIT, lp), lambda b, u: (b, u, 0, 0))
    k_spec = pl.BlockSpec((None, ups, lp, UNIT), lambda b, u: (b, u, 0, 0))
    if fox:
        extra_specs = [pl.BlockSpec((None, ups, 2, lp), lambda b, u: (b, u, 0, 0))]
    else:
        extra_specs = [pl.BlockSpec((4, DIFF_QK_DIM), lambda b, u: (0, 0)),
                       pl.BlockSpec((ups, 1, UNIT), lambda b, u: (u, 0, 0))]
    stat = pltpu.VMEM((ups, 2, 1, tq), F32)
    acc = pltpu.VMEM((ups, 2, UNIT + ONES_ROWS, tq), F32)
    score = pltpu.VMEM((ups, 2, tq, tq), F32)
    prob = pltpu.VMEM((ups, 2, tq, tq), BF16)
    return pl.pallas_call(
        functools.partial(_flash_kernel, fox=fox, tq=tq, seq=seq, lambda_init=lambda_init),
        grid=(bsz, N_UNITS // ups),
        in_specs=[q_spec, k_spec, q_spec] + extra_specs,
        out_specs=pl.BlockSpec((None, lp, ups * UNIT), lambda b, u: (b, 0, u)),
        out_shape=jax.ShapeDtypeStruct((bsz, lp, N_UNITS * UNIT), BF16),
        scratch_shapes=[stat, acc, acc, pltpu.VMEM((ups, 2, UNIT, tq), BF16),
                        score, score, prob, prob, stat, stat, stat, stat],
        compiler_params=pltpu.CompilerParams(
            dimension_semantics=("parallel", "parallel"), vmem_limit_bytes=VMEM_LIMIT),
        name="fox_flash" if fox else "diff_flash",
    )(qt, k, vt, *extra)


def _out_kernel(*refs, final, split_meta):
    oa_ref, ob_ref, z_ref, w_ref, h_ref = refs[:5]
    meta_ref = refs[5] if split_meta else None
    g_ref = refs[-2] if final else None
    o_ref = refs[-1]
    half = oa_ref.shape[-1]
    mixed_a = (oa_ref[...].astype(F32) * z_ref[:, 0:half]).astype(BF16)
    mixed_b = (ob_ref[...].astype(F32) * z_ref[:, half:2 * half]).astype(BF16)
    h = (_hidden_tile(h_ref, meta_ref, pl.program_id(1) == pl.num_programs(1) - 1)
         + jnp.dot(mixed_a, w_ref[0:half, :], preferred_element_type=F32)
         + jnp.dot(mixed_b, w_ref[half:2 * half, :], preferred_element_type=F32))
    if final:
        h = h * lax.rsqrt(jnp.mean(h * h, axis=-1, keepdims=True) + NORM_EPS) * g_ref[...]
    o_ref[...] = h


def _out_project(o_a, o_b, zg, w_out, h, meta_tile, final_g, *, tl, rows):
    bsz, _, dm = h.shape
    half = o_a.shape[-1]
    final = final_g is not None
    split_meta = meta_tile is not None
    nt = rows // tl
    row_spec = lambda width: pl.BlockSpec((None, tl, width), lambda b, i: (b, i, 0))
    const = lambda shape: pl.BlockSpec(shape, lambda b, i: (0,) * len(shape))
    in_specs = [row_spec(half), row_spec(half), row_spec(2 * half), const((2 * half, dm))]
    args = [o_a, o_b, zg, w_out, h]
    if split_meta:
        in_specs += [pl.BlockSpec((None, tl, dm), lambda b, i: (b, jnp.minimum(i, nt - 2), 0)),
                     const((tl, dm))]
        args.append(meta_tile)
    else:
        in_specs.append(row_spec(dm))
    if final:
        in_specs.append(const((1, dm)))
        args.append(final_g)
    return pl.pallas_call(
        functools.partial(_out_kernel, final=final, split_meta=split_meta),
        grid=(bsz, nt),
        in_specs=in_specs,
        out_specs=row_spec(dm),
        out_shape=jax.ShapeDtypeStruct((bsz, rows, dm), F32),
        compiler_params=pltpu.CompilerParams(
            dimension_semantics=("parallel", "parallel"), vmem_limit_bytes=VMEM_LIMIT),
        name="out_final" if final else "out_proj",
    )(*args)


def _rope_tables(seq, lp):
    half = DIFF_QK_DIM // 2
    pos = np.zeros((lp,), np.float64)
    pos[:seq] = N_META + np.arange(seq)
    pos[seq:seq + N_META] = np.arange(N_META)
    inv = ROPE_THETA ** (-np.arange(half, dtype=np.float64) / half)
    ang = pos[:, None] * np.tile(inv, UNIT // half)[None, :]
    sign = np.where(np.arange(UNIT) < UNIT // 2, -1.0, 1.0)[None, :]
    cos, sin = np.cos(ang), np.sin(ang) * sign
    q_scale = DIFF_QK_DIM ** -0.5 * LOG2E
    return tuple(jnp.asarray(t, F32) for t in (cos * q_scale, sin * q_scale, cos, sin))


def _unit_lane_order(w_qk):
    dm, width = w_qk.shape
    half = DIFF_QK_DIM // 2
    w5 = w_qk.reshape(dm, width // (4 * half), 2, 2, half)
    return w5.transpose(0, 1, 3, 2, 4).reshape(dm, width)


def _projection_weight(w_layer):
    width = N_UNITS * UNIT
    w_layer = w_layer.astype(BF16)
    sec = lambda k: w_layer[:, k * width:(k + 1) * width]
    logits = jnp.pad(w_layer[:, 8 * width:], ((0, 0), (0, UNIT - FOX_HEADS)))
    return jnp.concatenate(
        [_unit_lane_order(sec(0)), _unit_lane_order(sec(1)), sec(2), sec(4), sec(5), sec(6),
         sec(3), sec(7), logits], axis=1)


def kernel(x, meta_tokens, norm_g, w_in, b_forget, lam_q1, lam_k1, lam_q2, lam_k2, subln_g,
           w_out, final_g):
    bsz, seq, dm = x.shape
    depth = w_in.shape[0]
    tl = SEQ_TILE
    lp = seq + tl
    width = N_UNITS * UNIT
    assert seq % tl == 0 and N_META <= UNIT and w_in.shape[2] == 8 * width + FOX_HEADS

    h, meta_tile = x, jnp.pad(meta_tokens.astype(x.dtype), ((0, tl - N_META), (0, 0)))

    tabs = _rope_tables(seq, lp)
    tri = jnp.asarray(np.tril(np.ones((tl, tl), np.float32)), BF16)

    out = None
    for layer in range(depth):
        lambda_init = 0.8 - 0.6 * math.exp(-0.3 * layer)
        w = _projection_weight(w_in[layer])
        bf = jnp.pad(b_forget[layer].astype(F32), (0, UNIT - FOX_HEADS)).reshape(1, UNIT)
        (qdt, kd, vdt, qft, kf, vft, zg, cum) = _project(
            h, meta_tile, norm_g[layer].reshape(1, dm), w, bf, tabs, tri, tl=tl, lp=lp)

        lam = jnp.stack([lam_q1[layer], lam_k1[layer], lam_q2[layer], lam_k2[layer]]).astype(F32)
        o_a = _flash(qdt, kd, vdt, (lam, subln_g[layer].reshape(N_UNITS, 1, UNIT)),
                     fox=False, tq=tl, seq=seq, lambda_init=lambda_init)
        o_b = _flash(qft, kf, vft, (cum.reshape(bsz, N_UNITS, 2, lp),),
                     fox=True, tq=tl, seq=seq, lambda_init=lambda_init)

        w_o = w_out[layer].astype(BF16)
        if layer + 1 < depth:
            h = _out_project(o_a, o_b, zg, w_o, h, meta_tile, None, tl=tl, rows=lp)
            meta_tile = None
        else:
            out = _out_project(o_a, o_b, zg, w_o, h, None, final_g.reshape(1, dm),
                               tl=tl, rows=seq)
    return out
```

```python
import functools
import math

import numpy as np
import jax
import jax.numpy as jnp
from jax import lax
from jax.experimental import pallas as pl
from jax.experimental.pallas import tpu as pltpu

N_META = 16
DIFF_QK_DIM = 64
FOX_HEAD_DIM = 64
FOX_HEADS = 8
UNIT = 128
N_UNITS = 4
UNITS_PER_STEP = 2
ROPE_THETA = 10000.0
NORM_EPS = 1e-6
LOG2E = 1.0 / math.log(2.0)
NEG = -1e30
ONES_ROWS = 16

SEQ_TILE = 512
VMEM_LIMIT = 56 * 1024 * 1024

F32 = jnp.float32
BF16 = jnp.bfloat16


def _hidden_tile(h_ref, meta_ref, is_meta_tile):
    if meta_ref is None:
        return h_ref[...]
    return jnp.where(is_meta_tile, meta_ref[...], h_ref[...])


def _proj_kernel(*refs, tl, fox_scale, split_meta):
    h_ref, meta_ref = (refs[0], refs[1]) if split_meta else (refs[0], None)
    (g_ref, w_ref, bf_ref, cosq_ref, sinq_ref, cosk_ref, sink_ref, tri_ref,
     qdt_ref, kd_ref, vdt_ref, qft_ref, kf_ref, vft_ref, zg_ref, cum_ref,
     carry_ref) = refs[2 if split_meta else 1:]
    step = pl.program_id(1)

    @pl.when(step == 0)
    def _():
        carry_ref[...] = jnp.zeros_like(carry_ref)

    h = _hidden_tile(h_ref, meta_ref, step == 0)
    u = h * lax.rsqrt(jnp.mean(h * h, axis=-1, keepdims=True) + NORM_EPS) * g_ref[...]
    ub = u.astype(BF16)

    def proj(lo, width):
        return jnp.dot(ub, w_ref[:, lo:lo + width], preferred_element_type=F32)

    def rope(t, cos_ref, sin_ref):
        return t * cos_ref[...] + pltpu.roll(t, 64, 1) * sin_ref[...]

    width = N_UNITS * UNIT

    fl = proj(8 * width, UNIT) + bf_ref[...]
    ls = (jnp.minimum(fl, 0.0) - jnp.log1p(jnp.exp(-jnp.abs(fl)))) * LOG2E
    row = lax.broadcasted_iota(jnp.int32, (tl, UNIT), 0)
    ls = jnp.where(row >= jnp.where(step == 0, N_META, tl), 0.0, ls)

    hi = ls.astype(BF16)
    r1 = ls - hi.astype(F32)
    mid = r1.astype(BF16)
    lo = (r1 - mid.astype(F32)).astype(BF16)
    tri = tri_ref[...]
    cum = (jnp.dot(tri, hi, preferred_element_type=F32)
           + jnp.dot(tri, mid, preferred_element_type=F32)
           + jnp.dot(tri, lo, preferred_element_type=F32)) + carry_ref[...]
    carry_ref[...] = cum[tl - 1:tl, :]
    cum_ref[...] = cum.T[0:FOX_HEADS, :]

    dq = proj(0 * width, width)
    for un in range(N_UNITS):
        qdt_ref[un] = rope(dq[:, un * UNIT:(un + 1) * UNIT], cosq_ref, sinq_ref).T.astype(BF16)
    dk = proj(1 * width, width)
    for un in range(N_UNITS):
        kd_ref[un] = rope(dk[:, un * UNIT:(un + 1) * UNIT], cosk_ref, sink_ref).astype(BF16)
    dv = proj(2 * width, width)
    for un in range(N_UNITS):
        vdt_ref[un] = dv[:, un * UNIT:(un + 1) * UNIT].T.astype(BF16)
    fq = proj(3 * width, width) * fox_scale
    for un in range(N_UNITS):
        qft_ref[un] = fq[:, un * UNIT:(un + 1) * UNIT].T.astype(BF16)
    fk = proj(4 * width, width)
    for un in range(N_UNITS):
        kf_ref[un] = fk[:, un * UNIT:(un + 1) * UNIT].astype(BF16)
    fv = proj(5 * width, width)
    for un in range(N_UNITS):
        vft_ref[un] = fv[:, un * UNIT:(un + 1) * UNIT].T.astype(BF16)

    z = proj(6 * width, 2 * width)
    zg_ref[...] = (z * (1.0 / (1.0 + jnp.exp(-z)))).astype(BF16)


def _project(h, meta_tile, norm_g, w, b_forget, tabs, tri, *, tl, lp):
    bsz, _, dm = h.shape
    nl = lp // tl
    pw = w.shape[1]
    split_meta = meta_tile is not None

    def tok(b, i):
        return (i + nl - 1) % nl

    row_spec = lambda width: pl.BlockSpec((None, tl, width), lambda b, i: (b, tok(b, i), 0))
    unit_rows = pl.BlockSpec((None, N_UNITS, tl, UNIT), lambda b, i: (b, 0, tok(b, i), 0))
    unit_cols = pl.BlockSpec((None, N_UNITS, UNIT, tl), lambda b, i: (b, 0, 0, tok(b, i)))
    tab_spec = pl.BlockSpec((tl, UNIT), lambda b, i: (tok(b, i), 0))
    const = lambda shape: pl.BlockSpec(shape, lambda b, i: (0,) * len(shape))

    unit_t = jax.ShapeDtypeStruct((bsz, N_UNITS, UNIT, lp), BF16)
    unit_n = jax.ShapeDtypeStruct((bsz, N_UNITS, lp, UNIT), BF16)
    if split_meta:
        h_specs = [pl.BlockSpec((None, tl, dm), lambda b, i: (b, jnp.minimum(tok(b, i), nl - 2), 0)),
                   const((tl, dm))]
        h_args = [h, meta_tile]
    else:
        h_specs, h_args = [row_spec(dm)], [h]
    return pl.pallas_call(
        functools.partial(_proj_kernel, tl=tl, fox_scale=FOX_HEAD_DIM ** -0.5 * LOG2E,
                          split_meta=split_meta),
        grid=(bsz, nl),
        in_specs=h_specs + [const((1, dm)), const((dm, pw)), const((1, UNIT)),
                            tab_spec, tab_spec, tab_spec, tab_spec, const((tl, tl))],
        out_specs=[unit_cols, unit_rows, unit_cols, unit_cols, unit_rows, unit_cols,
                   row_spec(2 * N_UNITS * UNIT),
                   pl.BlockSpec((None, FOX_HEADS, tl), lambda b, i: (b, 0, tok(b, i)))],
        out_shape=[unit_t, unit_n, unit_t, unit_t, unit_n, unit_t,
                   jax.ShapeDtypeStruct((bsz, lp, 2 * N_UNITS * UNIT), BF16),
                   jax.ShapeDtypeStruct((bsz, FOX_HEADS, lp), F32)],
        scratch_shapes=[pltpu.VMEM((1, UNIT), F32)],
        compiler_params=pltpu.CompilerParams(
            dimension_semantics=("arbitrary", "arbitrary"), vmem_limit_bytes=VMEM_LIMIT),
        name="proj",
    )(*h_args, norm_g, w, b_forget, *tabs, tri)


def _flash_kernel(*refs, fox, tq, seq, lambda_init):
    if fox:
        qt_ref, k_ref, vt_ref, cum_ref, o_ref = refs[:5]
        n_in = 5
    else:
        qt_ref, k_ref, vt_ref, lam_ref, g_ref, o_ref = refs[:6]
        n_in = 6
    m_sc, acc_sc, acc0_sc, q_sc = refs[n_in:n_in + 4]
    s_buf, p_buf, alpha_buf, mblk_buf = (
        refs[n_in + 4 + 2 * n:n_in + 6 + 2 * n] for n in range(4))
    units = range(UNITS_PER_STEP)
    tk = tq
    nq = seq // tq
    n_steps = nq * (nq + 1) // 2

    def tile_start(t):
        return pl.multiple_of(t * tq, tq)

    def load_queries(un, i):
        feat = lax.broadcasted_iota(jnp.int32, (UNIT, tq), 0)
        second = (feat >= FOX_HEAD_DIM) if fox else ((feat // 32) % 2 == 1)
        qt = qt_ref[un, :, pl.ds(tile_start(i), tq)].astype(F32)
        q_sc[un, 0] = jnp.where(second, 0.0, qt).astype(BF16)
        q_sc[un, 1] = jnp.where(second, qt, 0.0).astype(BF16)

    all_lanes = slice(0, tq)
    lane_halves = (slice(0, tq // 2), slice(tq // 2, tq))
    plain_pieces = (all_lanes,) if fox else lane_halves

    def key_bias(un, k_off, k_len, c):
        if not fox:
            return None
        return jnp.broadcast_to(cum_ref[un, c:c + 1, pl.ds(k_off, k_len)], (UNIT, k_len)).T

    def scores(un, k_off, k_len, c, lanes=all_lanes, ck=None):
        s = jnp.dot(k_ref[un, pl.ds(k_off, k_len), :], q_sc[un, c, :, lanes],
                    preferred_element_type=F32)
        if fox:
            if ck is None:
                ck = key_bias(un, k_off, k_len, c)
            s = jnp.concatenate(
                [s[:, b * UNIT:(b + 1) * UNIT] - ck for b in range(s.shape[1] // UNIT)], axis=1)
        return s

    def issue_scores(un, j, par, cs=(0, 1), lanes=all_lanes, ck=None):
        for c in cs:
            s = scores(un, tile_start(j), tk, c, lanes, ck)
            s_buf[par][un, c, :, lanes] = s
            mblk_buf[par][un, c, :, lanes] = jnp.max(s, axis=0, keepdims=True)

    def softmax_core(un, i, c, s, m_blk, m_old, lanes=all_lanes):
        if fox:
            cq = cum_ref[un, c:c + 1, pl.ds(tile_start(i) + lanes.start, lanes.stop - lanes.start)]
            m_new = jnp.maximum(m_old, m_blk + cq)
            shift = m_new - cq
        else:
            m_new = jnp.maximum(m_old, m_blk)
            shift = m_new
        return m_new, jnp.exp2(m_old - m_new), jnp.exp2(s - shift)

    def softmax(un, i, par, cs=(0, 1), lanes=all_lanes):
        for c in cs:
            m_new, alpha, p = softmax_core(un, i, c, s_buf[par][un, c, :, lanes],
                                           mblk_buf[par][un, c, :, lanes],
                                           m_sc[un, c, :, lanes], lanes)
            alpha_buf[par][un, c, :, lanes] = alpha
            p_buf[par][un, c, :, lanes] = p.astype(BF16)
            m_sc[un, c, :, lanes] = m_new

    def softmax_diagonal(un, i, par, c):
        half = tq // 2
        below = (lax.broadcasted_iota(jnp.int32, (half, half), 0)
                 <= lax.broadcasted_iota(jnp.int32, (half, half), 1))
        for part, lanes in enumerate(lane_halves):
            s = jnp.where(below, s_buf[par][un, c, lanes, lanes], NEG)
            if part == 1:
                s = jnp.concatenate([s_buf[par][un, c, 0:half, lanes], s], axis=0)
            m_new, alpha, p = softmax_core(un, i, c, s, jnp.max(s, axis=0, keepdims=True),
                                           m_sc[un, c, :, lanes], lanes)
            alpha_buf[par][un, c, :, lanes] = alpha
            m_sc[un, c, :, lanes] = m_new
            if part == 0:
                p_buf[par][un, c, 0:half, lanes] = p.astype(BF16)
                p_buf[par][un, c, half:tk, lanes] = jnp.zeros((tk - half, half), BF16)
            else:
                p_buf[par][un, c, :, lanes] = p.astype(BF16)

    def pv_update(un, j, par, cs=(0, 1), lanes=all_lanes):
        vtt = jnp.concatenate(
            [vt_ref[un, :, pl.ds(tile_start(j), tk)], jnp.ones((ONES_ROWS, tk), BF16)], axis=0)
        for c in cs:
            acc_sc[un, c, :, lanes] = (
                alpha_buf[par][un, c, :, lanes] * acc_sc[un, c, :, lanes]
                + jnp.dot(vtt, p_buf[par][un, c, :, lanes], preferred_element_type=F32))

    def meta_scores(un, causal):
        key = lax.broadcasted_iota(jnp.int32, (UNIT, tq), 0)
        mask = key < N_META
        if causal:
            mask = jnp.logical_and(mask, key <= lax.broadcasted_iota(jnp.int32, (UNIT, tq), 1))
        return [jnp.where(mask, scores(un, seq, UNIT, c), NEG) for c in range(2)]

    def meta_softmax(un, i, s_pair):
        p_pair = []
        for c, s in enumerate(s_pair):
            m_new, _, p = softmax_core(un, i, c, s, jnp.max(s, axis=0, keepdims=True),
                                       jnp.full((1, tq), NEG, F32))
            m_sc[un, c] = m_new
            p_pair.append(p.astype(BF16))
        return p_pair

    def meta_values(un, p_pair):
        vtt = vt_ref[un, :, pl.ds(seq, UNIT)]
        for c, p in enumerate(p_pair):
            acc0_sc[un, c, 0:UNIT] = jnp.dot(vtt, p, preferred_element_type=F32)
            denom = jnp.sum(p.astype(F32), axis=0, keepdims=True)
            acc0_sc[un, c, UNIT:UNIT + ONES_ROWS] = jnp.broadcast_to(denom, (ONES_ROWS, tq))

    def start_row(un, i, causal):
        meta_values(un, meta_softmax(un, i, meta_scores(un, causal)))

    def normalized(un, c):
        acc = acc_sc[un, c]
        return acc[:UNIT] * (1.0 / acc[UNIT:UNIT + 1])

    def finalize(un, i):
        o1 = normalized(un, 0)
        o2 = normalized(un, 1)
        if fox:
            o = jnp.concatenate([o1[:FOX_HEAD_DIM], o2[FOX_HEAD_DIM:]], axis=0).T
        else:
            lam = (jnp.exp(jnp.sum(lam_ref[0:1, :] * lam_ref[1:2, :], axis=-1, keepdims=True))
                   - jnp.exp(jnp.sum(lam_ref[2:3, :] * lam_ref[3:4, :], axis=-1, keepdims=True))
                   + lambda_init)
            o = (o1 - lam * o2).T
            o = o * lax.rsqrt(jnp.mean(o * o, axis=-1, keepdims=True) + NORM_EPS) * g_ref[un]
            o = o * (1.0 - lambda_init)
        o_ref[pl.ds(tile_start(i), tq), un * UNIT:(un + 1) * UNIT] = o.astype(BF16)

    def diag_softmax(i, par):
        for un in units:
            for c in range(2):
                softmax_diagonal(un, i, par, c)
                pv_update(un, jnp.maximum(i - 1, 0), 1 - par, cs=(c,))

    def next_row_scores(i, par):
        for un in units:
            load_queries(un, i + 1)
        s_meta = [meta_scores(un, causal=False) for un in units]
        for un in units:
            issue_scores(un, 0, 1 - par)
        return s_meta

    def next_row_start(i, s_meta):
        p_meta = [meta_softmax(un, i + 1, s_meta[un]) for un in units]
        for un in units:
            meta_values(un, p_meta[un])

    def step(i, j, par):
        @pl.when(jnp.logical_and(j > 0, j < i))
        def _():
            for un in units:
                for c in range(2):
                    ck = key_bias(un, tile_start(j + 1), tk, c)
                    for lanes in plain_pieces:
                        softmax(un, i, par, cs=(c,), lanes=lanes)
                        pv_update(un, j - 1, 1 - par, cs=(c,), lanes=lanes)
                        issue_scores(un, j + 1, 1 - par, cs=(c,), lanes=lanes, ck=ck)

        @pl.when(jnp.logical_and(j == 0, i > 0))
        def _():
            for un in units:
                for c in range(2):
                    softmax(un, i, par, cs=(c,))
                    pv_update(un, i - 1, 1 - par, cs=(c,))
                    issue_scores(un, 1, 1 - par, cs=(c,))
            for un in units:
                finalize(un, i - 1)
                acc_sc[un] = acc0_sc[un]

        @pl.when(j == i)
        def _():
            s_meta = next_row_scores(i, par)
            diag_softmax(i, par)
            next_row_start(i, s_meta)

        wrap = j == i
        return jnp.where(wrap, i + 1, i), jnp.where(wrap, 0, j + 1)

    for un in units:
        load_queries(un, 0)
        start_row(un, 0, causal=False)
        acc_sc[un] = acc0_sc[un]
        issue_scores(un, 0, 0)
    alpha_buf[1][...] = jnp.ones_like(alpha_buf[1])
    p_buf[1][...] = jnp.zeros_like(p_buf[1])

    def body(_, ij):
        i, j = step(ij[0], ij[1], 0)
        return step(i, j, 1)

    i, j = lax.fori_loop(0, (n_steps - 2) // 2, body, (jnp.int32(0), jnp.int32(0)))
    step(i, j, 0)
    diag_softmax(nq - 1, 1)
    for un in units:
        pv_update(un, nq - 1, 1)
        finalize(un, nq - 1)

    for un in units:
        load_queries(un, nq)
        start_row(un, nq, causal=True)
        acc_sc[un] = acc0_sc[un]
        finalize(un, nq)


def _flash(qt, k, vt, extra, *, fox, tq, seq, lambda_init):
    bsz, _, lp, _ = k.shape
    nq = seq // tq
    ups = UNITS_PER_STEP
    assert lp == seq + tq and (nq * (nq + 1) // 2) % 2 == 0 and N_UNITS % ups == 0
    q_spec = pl.BlockSpec((None, ups, UNIT, lp), lambda b, u: (b, u, 0, 0))
    k_spec = pl.BlockSpec((None, ups, lp, UNIT), lambda b, u: (b, u, 0, 0))
    if fox:
        extra_specs = [pl.BlockSpec((None, ups, 2, lp), lambda b, u: (b, u, 0, 0))]
    else:
        extra_specs = [pl.BlockSpec((4, DIFF_QK_DIM), lambda b, u: (0, 0)),
                       pl.BlockSpec((ups, 1, UNIT), lambda b, u: (u, 0, 0))]
    stat = pltpu.VMEM((ups, 2, 1, tq), F32)
    acc = pltpu.VMEM((ups, 2, UNIT + ONES_ROWS, tq), F32)
    score = pltpu.VMEM((ups, 2, tq, tq), F32)
    prob = pltpu.VMEM((ups, 2, tq, tq), BF16)
    return pl.pallas_call(
        functools.partial(_flash_kernel, fox=fox, tq=tq, seq=seq, lambda_init=lambda_init),
        grid=(bsz, N_UNITS // ups),
        in_specs=[q_spec, k_spec, q_spec] + extra_specs,
        out_specs=pl.BlockSpec((None, lp, ups * UNIT), lambda b, u: (b, 0, u)),
        out_shape=jax.ShapeDtypeStruct((bsz, lp, N_UNITS * UNIT), BF16),
        scratch_shapes=[stat, acc, acc, pltpu.VMEM((ups, 2, UNIT, tq), BF16),
                        score, score, prob, prob, stat, stat, stat, stat],
        compiler_params=pltpu.CompilerParams(
            dimension_semantics=("parallel", "parallel"), vmem_limit_bytes=VMEM_LIMIT),
        name="fox_flash" if fox else "diff_flash",
    )(qt, k, vt, *extra)


def _out_kernel(*refs, final, split_meta):
    oa_ref, ob_ref, z_ref, w_ref, h_ref = refs[:5]
    meta_ref = refs[5] if split_meta else None
    g_ref = refs[-2] if final else None
    o_ref = refs[-1]
    half = oa_ref.shape[-1]
    mixed_a = (oa_ref[...].astype(F32) * z_ref[:, 0:half].astype(F32)).astype(BF16)
    mixed_b = (ob_ref[...].astype(F32) * z_ref[:, half:2 * half].astype(F32)).astype(BF16)
    h = (_hidden_tile(h_ref, meta_ref, pl.program_id(1) == pl.num_programs(1) - 1)
         + jnp.dot(mixed_a, w_ref[0:half, :], preferred_element_type=F32)
         + jnp.dot(mixed_b, w_ref[half:2 * half, :], preferred_element_type=F32))
    if final:
        h = h * lax.rsqrt(jnp.mean(h * h, axis=-1, keepdims=True) + NORM_EPS) * g_ref[...]
    o_ref[...] = h


def _out_project(o_a, o_b, zg, w_out, h, meta_tile, final_g, *, tl, rows):
    bsz, _, dm = h.shape
    half = o_a.shape[-1]
    final = final_g is not None
    split_meta = meta_tile is not None
    nt = rows // tl
    row_spec = lambda width: pl.BlockSpec((None, tl, width), lambda b, i: (b, i, 0))
    const = lambda shape: pl.BlockSpec(shape, lambda b, i: (0,) * len(shape))
    in_specs = [row_spec(half), row_spec(half), row_spec(2 * half), const((2 * half, dm))]
    args = [o_a, o_b, zg, w_out, h]
    if split_meta:
        in_specs += [pl.BlockSpec((None, tl, dm), lambda b, i: (b, jnp.minimum(i, nt - 2), 0)),
                     const((tl, dm))]
        args.append(meta_tile)
    else:
        in_specs.append(row_spec(dm))
    if final:
        in_specs.append(const((1, dm)))
        args.append(final_g)
    return pl.pallas_call(
        functools.partial(_out_kernel, final=final, split_meta=split_meta),
        grid=(bsz, nt),
        in_specs=in_specs,
        out_specs=row_spec(dm),
        out_shape=jax.ShapeDtypeStruct((bsz, rows, dm), F32),
        compiler_params=pltpu.CompilerParams(
            dimension_semantics=("parallel", "parallel"), vmem_limit_bytes=VMEM_LIMIT),
        name="out_final" if final else "out_proj",
    )(*args)


def _rope_tables(seq, lp):
    half = DIFF_QK_DIM // 2
    pos = np.zeros((lp,), np.float64)
    pos[:seq] = N_META + np.arange(seq)
    pos[seq:seq + N_META] = np.arange(N_META)
    inv = ROPE_THETA ** (-np.arange(half, dtype=np.float64) / half)
    ang = pos[:, None] * np.tile(inv, UNIT // half)[None, :]
    sign = np.where(np.arange(UNIT) < UNIT // 2, -1.0, 1.0)[None, :]
    cos, sin = np.cos(ang), np.sin(ang) * sign
    q_scale = DIFF_QK_DIM ** -0.5 * LOG2E
    return tuple(jnp.asarray(t, F32) for t in (cos * q_scale, sin * q_scale, cos, sin))


def _unit_lane_order(w_qk):
    dm, width = w_qk.shape
    half = DIFF_QK_DIM // 2
    w5 = w_qk.reshape(dm, width // (4 * half), 2, 2, half)
    return w5.transpose(0, 1, 3, 2, 4).reshape(dm, width)


def _projection_weight(w_layer):
    width = N_UNITS * UNIT
    w_layer = w_layer.astype(BF16)
    sec = lambda k: w_layer[:, k * width:(k + 1) * width]
    logits = jnp.pad(w_layer[:, 8 * width:], ((0, 0), (0, UNIT - FOX_HEADS)))
    return jnp.concatenate(
        [_unit_lane_order(sec(0)), _unit_lane_order(sec(1)), sec(2), sec(4), sec(5), sec(6),
         sec(3), sec(7), logits], axis=1)


def kernel(x, meta_tokens, norm_g, w_in, b_forget, lam_q1, lam_k1, lam_q2, lam_k2, subln_g,
           w_out, final_g):
    bsz, seq, dm = x.shape
    depth = w_in.shape[0]
    tl = SEQ_TILE
    lp = seq + tl
    width = N_UNITS * UNIT
    assert seq % tl == 0 and N_META <= UNIT and w_in.shape[2] == 8 * width + FOX_HEADS

    h, meta_tile = x, jnp.pad(meta_tokens.astype(x.dtype), ((0, tl - N_META), (0, 0)))

    tabs = _rope_tables(seq, lp)
    tri = jnp.asarray(np.tril(np.ones((tl, tl), np.float32)), BF16)

    out = None
    for layer in range(depth):
        lambda_init = 0.8 - 0.6 * math.exp(-0.3 * layer)
        w = _projection_weight(w_in[layer])
        bf = jnp.pad(b_forget[layer].astype(F32), (0, UNIT - FOX_HEADS)).reshape(1, UNIT)
        (qdt, kd, vdt, qft, kf, vft, zg, cum) = _project(
            h, meta_tile, norm_g[layer].reshape(1, dm), w, bf, tabs, tri, tl=tl, lp=lp)

        lam = jnp.stack([lam_q1[layer], lam_k1[layer], lam_q2[layer], lam_k2[layer]]).astype(F32)
        o_a = _flash(qdt, kd, vdt, (lam, subln_g[layer].reshape(N_UNITS, 1, UNIT)),
                     fox=False, tq=tl, seq=seq, lambda_init=lambda_init)
        o_b = _flash(qft, kf, vft, (cum.reshape(bsz, N_UNITS, 2, lp),),
                     fox=True, tq=tl, seq=seq, lambda_init=lambda_init)

        w_o = w_out[layer].astype(BF16)
        if layer + 1 < depth:
            h = _out_project(o_a, o_b, zg, w_o, h, meta_tile, None, tl=tl, rows=lp)
            meta_tile = None
        else:
            out = _out_project(o_a, o_b, zg, w_o, h, None, final_g.reshape(1, dm),
                               tl=tl, rows=seq)
    return out
```
